```python
import math, functools
import jax, jax.numpy as jnp
from jax import lax
import numpy as np

D_MODEL = 1024
BATCH = 8
SEQ = 2048
DEPTH = 2
DEC_BATCH = 128
DEC_SEQ = 1
PAST_LEN = 16384
PAGE_SIZE = 128

D_A = D_MODEL
H_A = 8
DK_A = D_A // H_A
DV_A = D_A // H_A
D_B = D_MODEL
P_B = 64
H_B = D_B // P_B
N_STATE = 128
G_B = 2
CONV_W = 4
D_CONV_A = 2 * D_A
D_CONV_B = D_B + 2 * G_B * N_STATE
D_MIX = D_A + D_B
SPLIT_SIZES = (D_A, D_A, D_A, D_A, H_A, H_A, D_B, D_CONV_B, H_B)
D_PROJ = sum(SPLIT_SIZES)
CHUNK = 128
PEER_HEADS = 8
N_KEYS = 128
N_EXPERTS = N_KEYS * N_KEYS
D_KEY = 256
PEER_TOPK = 16
PEER_BLOCK = 256
EPS = 1e-6

kernel_name = 'hymba_mlstm_ssd_peer_step'


def rmsnorm(x, g):
    xf = x.astype(jnp.float32)
    y = xf * lax.rsqrt(jnp.mean(xf * xf, axis=-1, keepdims=True) + EPS)
    return (y * g.astype(jnp.float32)).astype(x.dtype)


def head_layernorm(h, g):
    hc = h - jnp.mean(h, axis=-1, keepdims=True)
    y = hc * lax.rsqrt(jnp.mean(hc * hc, axis=-1, keepdims=True) + EPS)
    return y.reshape(h.shape[0], h.shape[1], -1) * g.astype(jnp.float32)


def causal_conv_silu(x, buf, w, b):
    T = x.shape[1]
    xp = jnp.concatenate([buf.astype(x.dtype), x], axis=1)
    y = b + w[0] * xp[:, 0:T]
    for j in range(1, CONV_W):
        y = y + w[j] * xp[:, j:j + T]
    return jax.nn.silu(y), xp[:, xp.shape[1] - (CONV_W - 1):]


def to_chunks(a, L):
    b, T = a.shape[0], a.shape[1]
    return jnp.moveaxis(a.reshape((b, T // L, L) + a.shape[2:]), 1, 0)


def from_chunks(a):
    a = jnp.moveaxis(a, 0, 1)
    return a.reshape((a.shape[0], a.shape[1] * a.shape[2]) + a.shape[3:])


def mlstm_chunk(carry, inp):
    C, n, m = carry
    q, k, v, ig, lf = inp
    L = q.shape[1]
    bt = jnp.moveaxis(jnp.cumsum(lf, axis=1), 1, 2)
    igt = jnp.moveaxis(ig, 1, 2)
    causal = jnp.tril(jnp.ones((L, L), dtype=bool))
    log_d = jnp.where(causal, bt[..., :, None] - bt[..., None, :] + igt[..., None, :], -jnp.inf)
    log_inter = bt + m[..., None]
    m_t = jnp.maximum(log_inter, jnp.max(log_d, axis=-1))
    dmat = jnp.exp(log_d - m_t[..., None])
    inter = jnp.exp(log_inter - m_t)
    s = jnp.einsum('blhd,bshd->bhls', q, k) * dmat
    inter_l = jnp.moveaxis(inter, 1, 2)[..., None]
    num = jnp.einsum('bhls,bshv->blhv', s, v) + jnp.einsum('blhd,bhdv->blhv', q, C) * inter_l
    den = jnp.sum(s, axis=-1) + jnp.einsum('blhd,bhd->bhl', q, n) * inter
    denom = jnp.maximum(jnp.abs(den), jnp.exp(-m_t))
    h = num / jnp.moveaxis(denom, 1, 2)[..., None]
    b_last = bt[..., -1]
    log_w = b_last[..., None] - bt + igt
    m_new = jnp.maximum(b_last + m, jnp.max(log_w, axis=-1))
    w = jnp.exp(log_w - m_new[..., None])
    decay = jnp.exp(b_last + m - m_new)
    C_new = decay[..., None, None] * C + jnp.einsum('bhs,bshd,bshv->bhdv', w, k, v)
    n_new = decay[..., None] * n + jnp.einsum('bhs,bshd->bhd', w, k)
    return (C_new, n_new, m_new), h


def ssd_chunk(A, S, inp):
    x, dt, Bm, Cm = inp
    L = x.shape[1]
    rep = H_B // G_B
    Bh = jnp.repeat(Bm, rep, axis=2)
    Ch = jnp.repeat(Cm, rep, axis=2)
    cum = jnp.cumsum(jnp.moveaxis(dt * A, 1, 2), axis=-1)
    causal = jnp.tril(jnp.ones((L, L), dtype=bool))
    decay = jnp.exp(jnp.where(causal, cum[..., :, None] - cum[..., None, :], -jnp.inf))
    xdt = x * dt[..., None]
    scores = jnp.einsum('blhn,bshn->bhls', Ch, Bh) * decay
    y = jnp.einsum('bhls,bshp->blhp', scores, xdt) \
        + jnp.einsum('blhn,bhpn->blhp', Ch, S) * jnp.moveaxis(jnp.exp(cum), 1, 2)[..., None]
    w = jnp.exp(cum[..., -1:] - cum)
    S_new = jnp.exp(cum[..., -1])[..., None, None] * S + jnp.einsum('bhs,bshp,bshn->bhpn', w, xdt, Bh)
    return S_new, y


def peer_ffn(h, w_query, sub_keys, peer_u, peer_v):
    bsz, T, D = h.shape
    flat = h.reshape(bsz * T, D)
    ntok = flat.shape[0]
    pad = (-ntok) % PEER_BLOCK
    blocks = jnp.pad(flat, ((0, pad), (0, 0))).reshape(-1, PEER_BLOCK, D)
    keys = sub_keys.astype(jnp.float32)

    def one_block(xb):
        qry = jnp.einsum('td,de->te', xb, w_query).astype(jnp.float32)
        qry = qry.reshape(PEER_BLOCK, PEER_HEADS, 2, D_KEY // 2)
        s = jnp.einsum('thjd,jkd->thjk', qry, keys)
        s_top, i_top = lax.top_k(s, PEER_TOPK)
        cand = s_top[:, :, 0, :, None] + s_top[:, :, 1, None, :]
        cand_idx = i_top[:, :, 0, :, None] * N_KEYS + i_top[:, :, 1, None, :]
        cand = cand.reshape(PEER_BLOCK, PEER_HEADS, PEER_TOPK * PEER_TOPK)
        cand_idx = cand_idx.reshape(PEER_BLOCK, PEER_HEADS, PEER_TOPK * PEER_TOPK)
        best, pos = lax.top_k(cand, PEER_TOPK)
        idx = jnp.take_along_axis(cand_idx, pos, axis=-1)
        gates = jax.nn.softmax(best, axis=-1)
        u = peer_u[idx]
        act = jax.nn.gelu(jnp.einsum('thkd,td->thk', u, xb).astype(jnp.float32), approximate=False)
        wts = (gates * act).astype(xb.dtype)
        return jnp.einsum('thk,thkd->td', wts, peer_v[idx])

    out = lax.map(one_block, blocks).reshape(-1, D)[:ntok]
    return out.reshape(bsz, T, D)


def hybrid_layer(x, c, st, wl):
    (w_ada, b_ada, norm1_g, norm2_g, w_in, gate_b, conv_a_w, conv_a_b, mlstm_norm_g,
     conv_b_w, conv_b_b, dt_bias, a_log, d_skip, ssm_norm_g, w_out,
     w_query, sub_keys, peer_u, peer_v) = wl
    C0, n0, m0, conv_a0, S0, conv_b0 = st
    f32 = jnp.float32
    bsz, T, _ = x.shape
    chunk = CHUNK if T % CHUNK == 0 else T
    mod = jnp.einsum('bd,de->be', jax.nn.silu(c), w_ada) + b_ada
    sh1, sc1, g1, sh2, sc2, g2 = jnp.split(mod[:, None, :], 6, axis=-1)
    h = rmsnorm(x, norm1_g) * (1 + sc1) + sh1
    proj = jnp.einsum('btd,de->bte', h, w_in)
    bounds = np.cumsum(SPLIT_SIZES)[:-1].tolist()
    q, k, v, o, ig, fg, z, xbc, dt = jnp.split(proj, bounds, axis=-1)

    qk, conv_a1 = causal_conv_silu(jnp.concatenate([q, k], axis=-1), conv_a0, conv_a_w, conv_a_b)
    q, k = jnp.split(qk, 2, axis=-1)
    qh = q.reshape(bsz, T, H_A, DK_A).astype(f32)
    kh = k.reshape(bsz, T, H_A, DK_A).astype(f32) * (DK_A ** -0.5)
    vh = v.reshape(bsz, T, H_A, DV_A).astype(f32)
    log_i = ig.astype(f32) + gate_b[0].astype(f32)
    log_f = jax.nn.log_sigmoid(fg.astype(f32) + gate_b[1].astype(f32))
    xs_a = tuple(to_chunks(a, chunk) for a in (qh, kh, vh, log_i, log_f))
    (C1, n1, m1), h_a = lax.scan(mlstm_chunk, (C0.astype(f32), n0.astype(f32), m0.astype(f32)), xs_a)
    h_a = head_layernorm(from_chunks(h_a), mlstm_norm_g)
    h_a = (h_a * jax.nn.sigmoid(o.astype(f32))).astype(x.dtype)

    xbc, conv_b1 = causal_conv_silu(xbc, conv_b0, conv_b_w, conv_b_b)
    xs_b, Bm, Cm = jnp.split(xbc, [D_B, D_B + G_B * N_STATE], axis=-1)
    xh = xs_b.reshape(bsz, T, H_B, P_B).astype(f32)
    Bm = Bm.reshape(bsz, T, G_B, N_STATE).astype(f32)
    Cm = Cm.reshape(bsz, T, G_B, N_STATE).astype(f32)
    dt = jax.nn.softplus(dt.astype(f32) + dt_bias.astype(f32))
    A = -jnp.exp(a_log.astype(f32))
    xs_s = tuple(to_chunks(a, chunk) for a in (xh, dt, Bm, Cm))
    S1, y_b = lax.scan(functools.partial(ssd_chunk, A), S0.astype(f32), xs_s)
    y_b = from_chunks(y_b) + d_skip.astype(f32)[:, None] * xh
    y_b = y_b.reshape(bsz, T, D_B) * jax.nn.silu(z.astype(f32))
    y_b = rmsnorm(y_b, ssm_norm_g).astype(x.dtype)

    mixed = jnp.einsum('bte,ed->btd', jnp.concatenate([h_a, y_b], axis=-1), w_out)
    x = x + g1 * mixed
    h2 = rmsnorm(x, norm2_g) * (1 + sc2) + sh2
    x = x + g2 * peer_ffn(h2, w_query, sub_keys, peer_u, peer_v)
    return x, (C1, n1, m1, conv_a1, S1, conv_b1)


def setup_inputs(seed: int = 0) -> dict:
    key = jax.random.key(seed)
    ks = iter(jax.random.split(key, 40))

    def nrm(shape, scale):
        return jax.random.normal(next(ks), shape, jnp.float32) * scale

    dt0 = jnp.exp(jax.random.uniform(next(ks), (DEPTH, H_B), jnp.float32,
                                     minval=math.log(1e-3), maxval=math.log(1e-1)))
    gate_b = jnp.stack([nrm((DEPTH, H_A), 0.1) - 2.0,
                        jnp.linspace(3.0, 6.0, H_A)[None, :] + nrm((DEPTH, H_A), 0.1)], axis=1)
    return {
        'x_prompt': nrm((BATCH, SEQ, D_MODEL), 1.0),
        'x_sample': nrm((DEC_BATCH, DEC_SEQ, D_MODEL), 1.0),
        'c_prompt': nrm((BATCH, D_MODEL), 1.0),
        'c_sample': nrm((DEC_BATCH, D_MODEL), 1.0),
        'state_mlstm_c': nrm((DEPTH, DEC_BATCH, H_A, DK_A, DV_A), 0.1),
        'state_mlstm_n': nrm((DEPTH, DEC_BATCH, H_A, DK_A), 0.5),
        'state_mlstm_m': nrm((DEPTH, DEC_BATCH, H_A), 1.0),
        'state_mlstm_conv': nrm((DEPTH, DEC_BATCH, CONV_W - 1, D_CONV_A), 1.0),
        'state_ssm': nrm((DEPTH, DEC_BATCH, H_B, P_B, N_STATE), 0.5),
        'state_ssm_conv': nrm((DEPTH, DEC_BATCH, CONV_W - 1, D_CONV_B), 1.0),
        'w_ada': nrm((DEPTH, D_MODEL, 6 * D_MODEL), 0.5 * D_MODEL ** -0.5),
        'b_ada': nrm((DEPTH, 6 * D_MODEL), 0.02),
        'norm1_g': 1.0 + nrm((DEPTH, D_MODEL), 0.02),
        'norm2_g': 1.0 + nrm((DEPTH, D_MODEL), 0.02),
        'w_in': nrm((DEPTH, D_MODEL, D_PROJ), D_MODEL ** -0.5),
        'mlstm_gate_b': gate_b,
        'conv_a_w': nrm((DEPTH, CONV_W, D_CONV_A), CONV_W ** -0.5),
        'conv_a_b': nrm((DEPTH, D_CONV_A), 0.02),
        'mlstm_norm_g': 1.0 + nrm((DEPTH, D_A), 0.02),
        'conv_b_w': nrm((DEPTH, CONV_W, D_CONV_B), CONV_W ** -0.5),
        'conv_b_b': nrm((DEPTH, D_CONV_B), 0.02),
        'dt_bias': dt0 + jnp.log(-jnp.expm1(-dt0)),
        'a_log': jnp.log(jax.random.uniform(next(ks), (DEPTH, H_B), jnp.float32, minval=1.0, maxval=16.0)),
        'd_skip': 1.0 + nrm((DEPTH, H_B), 0.1),
        'ssm_norm_g': 1.0 + nrm((DEPTH, D_B), 0.02),
        'w_out': nrm((DEPTH, D_MIX, D_MODEL), D_MIX ** -0.5),
        'peer_w_query': nrm((DEPTH, D_MODEL, PEER_HEADS * D_KEY), D_MODEL ** -0.5),
        'peer_sub_keys': nrm((DEPTH, 2, N_KEYS, D_KEY // 2), (D_KEY // 2) ** -0.5),
        'peer_u': nrm((DEPTH, N_EXPERTS, D_MODEL), D_MODEL ** -0.5),
        'peer_v': nrm((DEPTH, N_EXPERTS, D_MODEL), PEER_HEADS ** -0.5),
        'final_norm_g': 1.0 + nrm((D_MODEL,), 0.02),
    }


def reference(x_prompt, x_sample, c_prompt, c_sample, state_mlstm_c, state_mlstm_n, state_mlstm_m,
              state_mlstm_conv, state_ssm, state_ssm_conv, w_ada, b_ada, norm1_g, norm2_g, w_in,
              mlstm_gate_b, conv_a_w, conv_a_b, mlstm_norm_g, conv_b_w, conv_b_b, dt_bias, a_log,
              d_skip, ssm_norm_g, w_out, peer_w_query, peer_sub_keys, peer_u, peer_v, final_norm_g):
    f32 = jnp.float32
    bp = x_prompt.shape[0]
    zero_state = (jnp.zeros((bp, H_A, DK_A, DV_A), f32),
                  jnp.zeros((bp, H_A, DK_A), f32),
                  jnp.zeros((bp, H_A), f32),
                  jnp.zeros((bp, CONV_W - 1, D_CONV_A), x_prompt.dtype),
                  jnp.zeros((bp, H_B, P_B, N_STATE), f32),
                  jnp.zeros((bp, CONV_W - 1, D_CONV_B), x_prompt.dtype))
    hp, hs = x_prompt, x_sample
    new_p, new_s = [], []
    for l in range(DEPTH):
        wl = (w_ada[l], b_ada[l], norm1_g[l], norm2_g[l], w_in[l], mlstm_gate_b[l], conv_a_w[l],
              conv_a_b[l], mlstm_norm_g[l], conv_b_w[l], conv_b_b[l], dt_bias[l], a_log[l],
              d_skip[l], ssm_norm_g[l], w_out[l], peer_w_query[l], peer_sub_keys[l],
              peer_u[l], peer_v[l])
        hp, st_p = hybrid_layer(hp, c_prompt, zero_state, wl)
        st_in = (state_mlstm_c[l], state_mlstm_n[l], state_mlstm_m[l], state_mlstm_conv[l],
                 state_ssm[l], state_ssm_conv[l])
        hs, st_s = hybrid_layer(hs, c_sample, st_in, wl)
        new_p.append(st_p)
        new_s.append(st_s)
    y_prompt = rmsnorm(hp, final_norm_g)
    y_sample = rmsnorm(hs, final_norm_g)
    dtypes = (state_mlstm_c.dtype, state_mlstm_n.dtype, state_mlstm_m.dtype,
              state_mlstm_conv.dtype, state_ssm.dtype, state_ssm_conv.dtype)

    def stacked(states, i):
        return jnp.stack([s[i] for s in states], axis=0).astype(dtypes[i])

    return (y_prompt, y_sample,
            stacked(new_p, 0), stacked(new_p, 1), stacked(new_p, 2),
            stacked(new_p, 3), stacked(new_p, 4), stacked(new_p, 5),
            stacked(new_s, 0), stacked(new_s, 1), stacked(new_s, 2),
            stacked(new_s, 3), stacked(new_s, 4), stacked(new_s, 5))
```

```python
import math, functools
import jax, jax.numpy as jnp
from jax import lax
import numpy as np
from jax.experimental import pallas as pl
from jax.experimental.pallas import tpu as pltpu

D_MODEL = 1024
BATCH = 8
SEQ = 2048
DEPTH = 2
DEC_BATCH = 128
DEC_SEQ = 1

D_A = D_MODEL
H_A = 8
DK_A = D_A // H_A
DV_A = D_A // H_A
D_B = D_MODEL
P_B = 64
H_B = D_B // P_B
N_STATE = 128
G_B = 2
CONV_W = 4
D_CONV_A = 2 * D_A
D_CONV_B = D_B + 2 * G_B * N_STATE
D_MIX = D_A + D_B
SPLIT_SIZES = (D_A, D_A, D_A, D_A, H_A, H_A, D_B, D_CONV_B, H_B)
D_PROJ = sum(SPLIT_SIZES)
CHUNK = 128
PEER_HEADS = 8
N_KEYS = 128
N_EXPERTS = N_KEYS * N_KEYS
D_KEY = 256
PEER_TOPK = 16
PEER_BLOCK = 256
EPS = 1e-6


def rmsnorm(x, g):
    xf = x.astype(jnp.float32)
    y = xf * lax.rsqrt(jnp.mean(xf * xf, axis=-1, keepdims=True) + EPS)
    return (y * g.astype(jnp.float32)).astype(x.dtype)


def head_layernorm(h, g):
    hc = h - jnp.mean(h, axis=-1, keepdims=True)
    y = hc * lax.rsqrt(jnp.mean(hc * hc, axis=-1, keepdims=True) + EPS)
    return y.reshape(h.shape[0], h.shape[1], -1) * g.astype(jnp.float32)


def causal_conv_silu(x, buf, w, b):
    T = x.shape[1]
    xp = jnp.concatenate([buf.astype(x.dtype), x], axis=1)
    y = b + w[0] * xp[:, 0:T]
    for j in range(1, CONV_W):
        y = y + w[j] * xp[:, j:j + T]
    return jax.nn.silu(y), xp[:, xp.shape[1] - (CONV_W - 1):]


def to_chunks(a, L):
    b, T = a.shape[0], a.shape[1]
    return jnp.moveaxis(a.reshape((b, T // L, L) + a.shape[2:]), 1, 0)


def from_chunks(a):
    a = jnp.moveaxis(a, 0, 1)
    return a.reshape((a.shape[0], a.shape[1] * a.shape[2]) + a.shape[3:])


def mlstm_chunk(carry, inp):
    C, n, m = carry
    q, k, v, ig, lf = inp
    L = q.shape[1]
    bt = jnp.moveaxis(jnp.cumsum(lf, axis=1), 1, 2)
    igt = jnp.moveaxis(ig, 1, 2)
    causal = jnp.tril(jnp.ones((L, L), dtype=bool))
    log_d = jnp.where(causal, bt[..., :, None] - bt[..., None, :] + igt[..., None, :], -jnp.inf)
    log_inter = bt + m[..., None]
    m_t = jnp.maximum(log_inter, jnp.max(log_d, axis=-1))
    dmat = jnp.exp(log_d - m_t[..., None])
    inter = jnp.exp(log_inter - m_t)
    s = jnp.einsum('blhd,bshd->bhls', q, k) * dmat
    inter_l = jnp.moveaxis(inter, 1, 2)[..., None]
    num = jnp.einsum('bhls,bshv->blhv', s, v) + jnp.einsum('blhd,bhdv->blhv', q, C) * inter_l
    den = jnp.sum(s, axis=-1) + jnp.einsum('blhd,bhd->bhl', q, n) * inter
    denom = jnp.maximum(jnp.abs(den), jnp.exp(-m_t))
    h = num / jnp.moveaxis(denom, 1, 2)[..., None]
    b_last = bt[..., -1]
    log_w = b_last[..., None] - bt + igt
    m_new = jnp.maximum(b_last + m, jnp.max(log_w, axis=-1))
    w = jnp.exp(log_w - m_new[..., None])
    decay = jnp.exp(b_last + m - m_new)
    C_new = decay[..., None, None] * C + jnp.einsum('bhs,bshd,bshv->bhdv', w, k, v)
    n_new = decay[..., None] * n + jnp.einsum('bhs,bshd->bhd', w, k)
    return (C_new, n_new, m_new), h


def ssd_chunk(A, S, inp):
    x, dt, Bm, Cm = inp
    L = x.shape[1]
    rep = H_B // G_B
    Bh = jnp.repeat(Bm, rep, axis=2)
    Ch = jnp.repeat(Cm, rep, axis=2)
    cum = jnp.cumsum(jnp.moveaxis(dt * A, 1, 2), axis=-1)
    causal = jnp.tril(jnp.ones((L, L), dtype=bool))
    decay = jnp.exp(jnp.where(causal, cum[..., :, None] - cum[..., None, :], -jnp.inf))
    xdt = x * dt[..., None]
    scores = jnp.einsum('blhn,bshn->bhls', Ch, Bh) * decay
    y = jnp.einsum('bhls,bshp->blhp', scores, xdt) \
        + jnp.einsum('blhn,bhpn->blhp', Ch, S) * jnp.moveaxis(jnp.exp(cum), 1, 2)[..., None]
    w = jnp.exp(cum[..., -1:] - cum)
    S_new = jnp.exp(cum[..., -1])[..., None, None] * S + jnp.einsum('bhs,bshp,bshn->bhpn', w, xdt, Bh)
    return S_new, y


def peer_ffn(h, w_query, sub_keys, peer_u, peer_v):
    bsz, T, D = h.shape
    flat = h.reshape(bsz * T, D)
    ntok = flat.shape[0]
    pad = (-ntok) % PEER_BLOCK
    blocks = jnp.pad(flat, ((0, pad), (0, 0))).reshape(-1, PEER_BLOCK, D)
    keys = sub_keys.astype(jnp.float32)

    def one_block(xb):
        qry = jnp.einsum('td,de->te', xb, w_query).astype(jnp.float32)
        qry = qry.reshape(PEER_BLOCK, PEER_HEADS, 2, D_KEY // 2)
        s = jnp.einsum('thjd,jkd->thjk', qry, keys)
        s_top, i_top = lax.top_k(s, PEER_TOPK)
        cand = s_top[:, :, 0, :, None] + s_top[:, :, 1, None, :]
        cand_idx = i_top[:, :, 0, :, None] * N_KEYS + i_top[:, :, 1, None, :]
        cand = cand.reshape(PEER_BLOCK, PEER_HEADS, PEER_TOPK * PEER_TOPK)
        cand_idx = cand_idx.reshape(PEER_BLOCK, PEER_HEADS, PEER_TOPK * PEER_TOPK)
        best, pos = lax.top_k(cand, PEER_TOPK)
        idx = jnp.take_along_axis(cand_idx, pos, axis=-1)
        gates = jax.nn.softmax(best, axis=-1)
        u = peer_u[idx]
        act = jax.nn.gelu(jnp.einsum('thkd,td->thk', u, xb).astype(jnp.float32), approximate=False)
        wts = (gates * act).astype(xb.dtype)
        return jnp.einsum('thk,thkd->td', wts, peer_v[idx])

    out = lax.map(one_block, blocks).reshape(-1, D)[:ntok]
    return out.reshape(bsz, T, D)


def hybrid_layer(x, c, st, wl):
    (w_ada, b_ada, norm1_g, norm2_g, w_in, gate_b, conv_a_w, conv_a_b, mlstm_norm_g,
     conv_b_w, conv_b_b, dt_bias, a_log, d_skip, ssm_norm_g, w_out,
     w_query, sub_keys, peer_u, peer_v) = wl
    C0, n0, m0, conv_a0, S0, conv_b0 = st
    f32 = jnp.float32
    bsz, T, _ = x.shape
    chunk = CHUNK if T % CHUNK == 0 else T
    mod = jnp.einsum('bd,de->be', jax.nn.silu(c), w_ada) + b_ada
    sh1, sc1, g1, sh2, sc2, g2 = jnp.split(mod[:, None, :], 6, axis=-1)
    h = rmsnorm(x, norm1_g) * (1 + sc1) + sh1
    proj = jnp.einsum('btd,de->bte', h, w_in)
    bounds = np.cumsum(SPLIT_SIZES)[:-1].tolist()
    q, k, v, o, ig, fg, z, xbc, dt = jnp.split(proj, bounds, axis=-1)

    qk, conv_a1 = causal_conv_silu(jnp.concatenate([q, k], axis=-1), conv_a0, conv_a_w, conv_a_b)
    q, k = jnp.split(qk, 2, axis=-1)
    qh = q.reshape(bsz, T, H_A, DK_A).astype(f32)
    kh = k.reshape(bsz, T, H_A, DK_A).astype(f32) * (DK_A ** -0.5)
    vh = v.reshape(bsz, T, H_A, DV_A).astype(f32)
    log_i = ig.astype(f32) + gate_b[0].astype(f32)
    log_f = jax.nn.log_sigmoid(fg.astype(f32) + gate_b[1].astype(f32))
    xs_a = tuple(to_chunks(a, chunk) for a in (qh, kh, vh, log_i, log_f))
    (C1, n1, m1), h_a = lax.scan(mlstm_chunk, (C0.astype(f32), n0.astype(f32), m0.astype(f32)), xs_a)
    h_a = head_layernorm(from_chunks(h_a), mlstm_norm_g)
    h_a = (h_a * jax.nn.sigmoid(o.astype(f32))).astype(x.dtype)

    xbc, conv_b1 = causal_conv_silu(xbc, conv_b0, conv_b_w, conv_b_b)
    xs_b, Bm, Cm = jnp.split(xbc, [D_B, D_B + G_B * N_STATE], axis=-1)
    xh = xs_b.reshape(bsz, T, H_B, P_B).astype(f32)
    Bm = Bm.reshape(bsz, T, G_B, N_STATE).astype(f32)
    Cm = Cm.reshape(bsz, T, G_B, N_STATE).astype(f32)
    dt = jax.nn.softplus(dt.astype(f32) + dt_bias.astype(f32))
    A = -jnp.exp(a_log.astype(f32))
    xs_s = tuple(to_chunks(a, chunk) for a in (xh, dt, Bm, Cm))
    S1, y_b = lax.scan(functools.partial(ssd_chunk, A), S0.astype(f32), xs_s)
    y_b = from_chunks(y_b) + d_skip.astype(f32)[:, None] * xh
    y_b = y_b.reshape(bsz, T, D_B) * jax.nn.silu(z.astype(f32))
    y_b = rmsnorm(y_b, ssm_norm_g).astype(x.dtype)

    mixed = jnp.einsum('bte,ed->btd', jnp.concatenate([h_a, y_b], axis=-1), w_out)
    x = x + g1 * mixed
    h2 = rmsnorm(x, norm2_g) * (1 + sc2) + sh2
    x = x + g2 * peer_ffn(h2, w_query, sub_keys, peer_u, peer_v)
    return x, (C1, n1, m1, conv_a1, S1, conv_b1)


def _final_norm_kernel(x_ref, g_ref, o_ref):
    x = x_ref[...]
    y = x * lax.rsqrt(jnp.mean(x * x, axis=-1, keepdims=True) + EPS)
    o_ref[...] = y * g_ref[...]


def _final_norm(x, g):
    shp = x.shape
    x2 = x.reshape(-1, shp[-1])
    n = x2.shape[0]
    tm = min(n, 256)
    out = pl.pallas_call(
        _final_norm_kernel,
        grid=(n // tm,),
        in_specs=[pl.BlockSpec((tm, shp[-1]), lambda i: (i, 0)),
                  pl.BlockSpec((1, shp[-1]), lambda i: (0, 0))],
        out_specs=pl.BlockSpec((tm, shp[-1]), lambda i: (i, 0)),
        out_shape=jax.ShapeDtypeStruct(x2.shape, x2.dtype),
    )(x2, g.reshape(1, -1))
    return out.reshape(shp)


def kernel(x_prompt, x_sample, c_prompt, c_sample, state_mlstm_c, state_mlstm_n, state_mlstm_m, state_mlstm_conv, state_ssm, state_ssm_conv, w_ada, b_ada, norm1_g, norm2_g, w_in, mlstm_gate_b, conv_a_w, conv_a_b, mlstm_norm_g, conv_b_w, conv_b_b, dt_bias, a_log, d_skip, ssm_norm_g, w_out, peer_w_query, peer_sub_keys, peer_u, peer_v, final_norm_g):
    f32 = jnp.float32
    bp = x_prompt.shape[0]
    zero_state = (jnp.zeros((bp, H_A, DK_A, DV_A), f32),
                  jnp.zeros((bp, H_A, DK_A), f32),
                  jnp.zeros((bp, H_A), f32),
                  jnp.zeros((bp, CONV_W - 1, D_CONV_A), x_prompt.dtype),
                  jnp.zeros((bp, H_B, P_B, N_STATE), f32),
                  jnp.zeros((bp, CONV_W - 1, D_CONV_B), x_prompt.dtype))
    hp, hs = x_prompt, x_sample
    new_p, new_s = [], []
    for l in range(DEPTH):
        wl = (w_ada[l], b_ada[l], norm1_g[l], norm2_g[l], w_in[l], mlstm_gate_b[l], conv_a_w[l],
              conv_a_b[l], mlstm_norm_g[l], conv_b_w[l], conv_b_b[l], dt_bias[l], a_log[l],
              d_skip[l], ssm_norm_g[l], w_out[l], peer_w_query[l], peer_sub_keys[l],
              peer_u[l], peer_v[l])
        hp, st_p = hybrid_layer(hp, c_prompt, zero_state, wl)
        st_in = (state_mlstm_c[l], state_mlstm_n[l], state_mlstm_m[l], state_mlstm_conv[l],
                 state_ssm[l], state_ssm_conv[l])
        hs, st_s = hybrid_layer(hs, c_sample, st_in, wl)
        new_p.append(st_p)
        new_s.append(st_s)
    y_prompt = _final_norm(hp, final_norm_g)
    y_sample = _final_norm(hs, final_norm_g)
    dtypes = (state_mlstm_c.dtype, state_mlstm_n.dtype, state_mlstm_m.dtype,
              state_mlstm_conv.dtype, state_ssm.dtype, state_ssm_conv.dtype)

    def stacked(states, i):
        return jnp.stack([s[i] for s in states], axis=0).astype(dtypes[i])

    return (y_prompt, y_sample,
            stacked(new_p, 0), stacked(new_p, 1), stacked(new_p, 2),
            stacked(new_p, 3), stacked(new_p, 4), stacked(new_p, 5),
            stacked(new_s, 0), stacked(new_s, 1), stacked(new_s, 2),
            stacked(new_s, 3), stacked(new_s, 4), stacked(new_s, 5))
```

```python
import math, functools
import jax, jax.numpy as jnp
from jax import lax
import numpy as np
from jax.experimental import pallas as pl
from jax.experimental.pallas import tpu as pltpu

D_MODEL = 1024
BATCH = 8
SEQ = 2048
DEPTH = 2
DEC_BATCH = 128
DEC_SEQ = 1

D_A = D_MODEL
H_A = 8
DK_A = D_A // H_A
DV_A = D_A // H_A
D_B = D_MODEL
P_B = 64
H_B = D_B // P_B
N_STATE = 128
G_B = 2
CONV_W = 4
D_CONV_A = 2 * D_A
D_CONV_B = D_B + 2 * G_B * N_STATE
D_MIX = D_A + D_B
SPLIT_SIZES = (D_A, D_A, D_A, D_A, H_A, H_A, D_B, D_CONV_B, H_B)
D_PROJ = sum(SPLIT_SIZES)
CHUNK = 128
PEER_HEADS = 8
N_KEYS = 128
N_EXPERTS = N_KEYS * N_KEYS
D_KEY = 256
PEER_TOPK = 16
PEER_BLOCK = 256
EPS = 1e-6


def rmsnorm(x, g):
    xf = x.astype(jnp.float32)
    y = xf * lax.rsqrt(jnp.mean(xf * xf, axis=-1, keepdims=True) + EPS)
    return (y * g.astype(jnp.float32)).astype(x.dtype)


def head_layernorm(h, g):
    hc = h - jnp.mean(h, axis=-1, keepdims=True)
    y = hc * lax.rsqrt(jnp.mean(hc * hc, axis=-1, keepdims=True) + EPS)
    return y.reshape(h.shape[0], h.shape[1], -1) * g.astype(jnp.float32)


def causal_conv_silu(x, buf, w, b):
    T = x.shape[1]
    xp = jnp.concatenate([buf.astype(x.dtype), x], axis=1)
    y = b + w[0] * xp[:, 0:T]
    for j in range(1, CONV_W):
        y = y + w[j] * xp[:, j:j + T]
    return jax.nn.silu(y), xp[:, xp.shape[1] - (CONV_W - 1):]


def to_chunks(a, L):
    b, T = a.shape[0], a.shape[1]
    return jnp.moveaxis(a.reshape((b, T // L, L) + a.shape[2:]), 1, 0)


def from_chunks(a):
    a = jnp.moveaxis(a, 0, 1)
    return a.reshape((a.shape[0], a.shape[1] * a.shape[2]) + a.shape[3:])


def mlstm_chunk(carry, inp):
    C, n, m = carry
    q, k, v, ig, lf = inp
    L = q.shape[1]
    bt = jnp.moveaxis(jnp.cumsum(lf, axis=1), 1, 2)
    igt = jnp.moveaxis(ig, 1, 2)
    causal = jnp.tril(jnp.ones((L, L), dtype=bool))
    log_d = jnp.where(causal, bt[..., :, None] - bt[..., None, :] + igt[..., None, :], -jnp.inf)
    log_inter = bt + m[..., None]
    m_t = jnp.maximum(log_inter, jnp.max(log_d, axis=-1))
    dmat = jnp.exp(log_d - m_t[..., None])
    inter = jnp.exp(log_inter - m_t)
    s = jnp.einsum('blhd,bshd->bhls', q, k) * dmat
    inter_l = jnp.moveaxis(inter, 1, 2)[..., None]
    num = jnp.einsum('bhls,bshv->blhv', s, v) + jnp.einsum('blhd,bhdv->blhv', q, C) * inter_l
    den = jnp.sum(s, axis=-1) + jnp.einsum('blhd,bhd->bhl', q, n) * inter
    denom = jnp.maximum(jnp.abs(den), jnp.exp(-m_t))
    h = num / jnp.moveaxis(denom, 1, 2)[..., None]
    b_last = bt[..., -1]
    log_w = b_last[..., None] - bt + igt
    m_new = jnp.maximum(b_last + m, jnp.max(log_w, axis=-1))
    w = jnp.exp(log_w - m_new[..., None])
    decay = jnp.exp(b_last + m - m_new)
    C_new = decay[..., None, None] * C + jnp.einsum('bhs,bshd,bshv->bhdv', w, k, v)
    n_new = decay[..., None] * n + jnp.einsum('bhs,bshd->bhd', w, k)
    return (C_new, n_new, m_new), h


def ssd_chunk(A, S, inp):
    x, dt, Bm, Cm = inp
    L = x.shape[1]
    rep = H_B // G_B
    Bh = jnp.repeat(Bm, rep, axis=2)
    Ch = jnp.repeat(Cm, rep, axis=2)
    cum = jnp.cumsum(jnp.moveaxis(dt * A, 1, 2), axis=-1)
    causal = jnp.tril(jnp.ones((L, L), dtype=bool))
    decay = jnp.exp(jnp.where(causal, cum[..., :, None] - cum[..., None, :], -jnp.inf))
    xdt = x * dt[..., None]
    scores = jnp.einsum('blhn,bshn->bhls', Ch, Bh) * decay
    y = jnp.einsum('bhls,bshp->blhp', scores, xdt) \
        + jnp.einsum('blhn,bhpn->blhp', Ch, S) * jnp.moveaxis(jnp.exp(cum), 1, 2)[..., None]
    w = jnp.exp(cum[..., -1:] - cum)
    S_new = jnp.exp(cum[..., -1])[..., None, None] * S + jnp.einsum('bhs,bshp,bshn->bhpn', w, xdt, Bh)
    return S_new, y


SUB = 128
NEG_INF = float('-inf')


def _topk_rank(s, vals_ref):
    iota = lax.broadcasted_iota(jnp.int32, s.shape, 0)
    rank = jnp.full(s.shape, float(PEER_TOPK), jnp.float32)
    work = s
    for i in range(PEER_TOPK):
        m = jnp.max(work, axis=0, keepdims=True)
        idx = jnp.min(jnp.where(work == m, iota, N_KEYS), axis=0, keepdims=True)
        first = iota == idx
        rank = jnp.where(first, float(i), rank)
        work = jnp.where(first, NEG_INF, work)
        vals_ref[i:i + 1, :] = m
    return rank


def _staircase(v1_ref, v2_ref):
    v1 = v1_ref[...]
    a16 = lax.broadcasted_iota(jnp.int32, v1.shape, 0).astype(jnp.float32) * float(PEER_TOPK)
    L = jnp.zeros(v1.shape, jnp.float32)
    for _ in range(PEER_TOPK):
        v2sel = jnp.full(v1.shape, NEG_INF, jnp.float32)
        for b in range(PEER_TOPK):
            v2sel = jnp.where(L == float(b), v2_ref[b:b + 1, :], v2sel)
        f = v1 + v2sel
        m = jnp.max(f, axis=0, keepdims=True)
        pos = a16 + L
        eq = f == m
        p = jnp.min(jnp.where(eq, pos, 1e9), axis=0, keepdims=True)
        L = L + jnp.where(eq & (pos == p), 1.0, 0.0)
    return L


def _peer_select_kernel(h2_ref, wqt_ref, keys_ref, r2_ref, lim_ref, e1_ref, e2_ref,
                        q_scr, v1_scr, v2_scr, l_scr):
    tT = h2_ref.shape[0]
    qt = lax.dot_general(wqt_ref[...], h2_ref[...], (((1,), (1,)), ((), ())),
                         preferred_element_type=jnp.float32)
    q_scr[...] = qt.astype(q_scr.dtype)

    def head_body(h, carry):
        base = pl.multiple_of(h * D_KEY, D_KEY)
        s1 = jnp.dot(keys_ref[0], q_scr[pl.ds(base, N_KEYS), :],
                     preferred_element_type=jnp.float32)
        s2 = jnp.dot(keys_ref[1], q_scr[pl.ds(base + N_KEYS, N_KEYS), :],
                     preferred_element_type=jnp.float32)
        for sub in range(tT // SUB):
            sl = slice(sub * SUB, (sub + 1) * SUB)
            s1b, s2b = s1[:, sl], s2[:, sl]
            rank1 = _topk_rank(s1b, v1_scr)
            rank2 = _topk_rank(s2b, v2_scr)
            L = _staircase(v1_scr, v2_scr)
            l_scr[...] = L
            ev1 = jnp.exp(v1_scr[...] - v1_scr[0:1, :])
            csum = jnp.zeros(L.shape, jnp.float32)
            for b in range(PEER_TOPK):
                eb = jnp.exp(v2_scr[b:b + 1, :] - v2_scr[0:1, :])
                csum = csum + jnp.where(L > float(b), eb, 0.0)
            z = jnp.sum(ev1 * csum, axis=0, keepdims=True)
            lim = jnp.zeros(rank1.shape, jnp.float32)
            for a in range(PEER_TOPK):
                lim = jnp.where(rank1 == float(a), l_scr[a:a + 1, :], lim)
            r2_ref[h, :, sl] = rank2
            lim_ref[h, :, sl] = lim
            e1_ref[h, :, sl] = jnp.exp(s1b - v1_scr[0:1, :]) / z
            e2_ref[h, :, sl] = jnp.exp(s2b - v2_scr[0:1, :])
        return carry

    lax.fori_loop(0, PEER_HEADS, head_body, 0)


def _peer_dense_kernel(h2_ref, r2_ref, lim_ref, e1_ref, e2_ref, u_ref, vt_ref, x_ref, g2_ref,
                       out_ref, acc_ref, at_scr, w_scr):
    j = pl.program_id(1)
    tE, tT = at_scr.shape
    n_e1 = tE // N_KEYS

    @pl.when(j == 0)
    def _():
        acc_ref[...] = jnp.zeros_like(acc_ref)

    at_scr[...] = lax.dot_general(u_ref[...], h2_ref[...], (((1,), (1,)), ((), ())),
                                  preferred_element_type=jnp.float32)
    for c in range(n_e1):
        e1 = j * n_e1 + c
        rows = slice(c * N_KEYS, (c + 1) * N_KEYS)
        lim_rows = [lim_ref[h, pl.ds(e1, 1), :] for h in range(PEER_HEADS)]
        e1_rows = [e1_ref[h, pl.ds(e1, 1), :] for h in range(PEER_HEADS)]
        for s in range(tT // SUB):
            cols = slice(s * SUB, (s + 1) * SUB)
            g = None
            for h in range(PEER_HEADS):
                sel = r2_ref[h, :, cols] < lim_rows[h][:, cols]
                t = jnp.where(sel, e2_ref[h, :, cols] * e1_rows[h][:, cols], 0.0)
                g = t if g is None else g + t
            a = at_scr[rows, cols]
            act = 0.5 * a * (1.0 + lax.erf(a * (2.0 ** -0.5)))
            w_scr[rows, cols] = (act * g).astype(w_scr.dtype)
    acc_ref[...] += jnp.dot(vt_ref[...], w_scr[...], preferred_element_type=jnp.float32)

    @pl.when(j == pl.num_programs(1) - 1)
    def _():
        out_ref[...] = x_ref[...] + g2_ref[...] * acc_ref[...].T


def _peer_block_sizes(ntok):
    t_sel = min(ntok, 256)
    t_dense = min(ntok, 512)
    return t_sel, t_dense, 512


def peer_residual(x, h2, g2, rows_per_g2, wqt, keys, u, vt):
    ntok, D = x.shape
    t_sel, t_dense, t_e = _peer_block_sizes(ntok)
    mxu_dt = u.dtype
    h2b = h2.astype(mxu_dt)
    sel_shape = jax.ShapeDtypeStruct((PEER_HEADS, N_KEYS, ntok), jnp.float32)
    sel_spec = pl.BlockSpec((PEER_HEADS, N_KEYS, t_sel), lambda i: (0, 0, i))
    r2, lim, e1, e2 = pl.pallas_call(
        _peer_select_kernel,
        grid=(ntok // t_sel,),
        in_specs=[pl.BlockSpec((t_sel, D), lambda i: (i, 0)),
                  pl.BlockSpec(wqt.shape, lambda i: (0, 0)),
                  pl.BlockSpec(keys.shape, lambda i: (0, 0, 0))],
        out_specs=[sel_spec] * 4,
        out_shape=[sel_shape] * 4,
        scratch_shapes=[pltpu.VMEM((PEER_HEADS * D_KEY, t_sel), mxu_dt),
                        pltpu.VMEM((PEER_TOPK, SUB), jnp.float32),
                        pltpu.VMEM((PEER_TOPK, SUB), jnp.float32),
                        pltpu.VMEM((PEER_TOPK, SUB), jnp.float32)],
        compiler_params=pltpu.CompilerParams(dimension_semantics=("arbitrary",)),
        name="peer_select",
    )(h2b, wqt, keys)

    if rows_per_g2 == 1:
        g2_spec = pl.BlockSpec((t_dense, D), lambda i, j: (i, 0))
    else:
        assert rows_per_g2 % t_dense == 0
        g2 = g2.reshape(-1, 1, D)
        g2_spec = pl.BlockSpec((None, 1, D), lambda i, j: (i // (rows_per_g2 // t_dense), 0, 0))
    tok_spec = pl.BlockSpec((PEER_HEADS, N_KEYS, t_dense), lambda i, j: (0, 0, i))
    return pl.pallas_call(
        _peer_dense_kernel,
        grid=(ntok // t_dense, N_EXPERTS // t_e),
        in_specs=[pl.BlockSpec((t_dense, D), lambda i, j: (i, 0)),
                  tok_spec, tok_spec, tok_spec, tok_spec,
                  pl.BlockSpec((t_e, D), lambda i, j: (j, 0)),
                  pl.BlockSpec((D, t_e), lambda i, j: (0, j)),
                  pl.BlockSpec((t_dense, D), lambda i, j: (i, 0)),
                  g2_spec],
        out_specs=pl.BlockSpec((t_dense, D), lambda i, j: (i, 0)),
        out_shape=jax.ShapeDtypeStruct((ntok, D), jnp.float32),
        scratch_shapes=[pltpu.VMEM((D, t_dense), jnp.float32),
                        pltpu.VMEM((t_e, t_dense), jnp.float32),
                        pltpu.VMEM((t_e, t_dense), mxu_dt)],
        compiler_params=pltpu.CompilerParams(
            dimension_semantics=("arbitrary", "arbitrary"),
            vmem_limit_bytes=48 * 1024 * 1024),
        name="peer_dense",
    )(h2b, r2, lim, e1, e2, u, vt, x, g2)


def hybrid_layer(x, c, st, wl):
    (w_ada, b_ada, norm1_g, norm2_g, w_in, gate_b, conv_a_w, conv_a_b, mlstm_norm_g,
     conv_b_w, conv_b_b, dt_bias, a_log, d_skip, ssm_norm_g, w_out, peer_w) = wl
    C0, n0, m0, conv_a0, S0, conv_b0 = st
    f32 = jnp.float32
    bsz, T, _ = x.shape
    chunk = CHUNK if T % CHUNK == 0 else T
    mod = jnp.einsum('bd,de->be', jax.nn.silu(c), w_ada) + b_ada
    sh1, sc1, g1, sh2, sc2, g2 = jnp.split(mod[:, None, :], 6, axis=-1)
    h = rmsnorm(x, norm1_g) * (1 + sc1) + sh1
    proj = jnp.einsum('btd,de->bte', h, w_in)
    bounds = np.cumsum(SPLIT_SIZES)[:-1].tolist()
    q, k, v, o, ig, fg, z, xbc, dt = jnp.split(proj, bounds, axis=-1)

    qk, conv_a1 = causal_conv_silu(jnp.concatenate([q, k], axis=-1), conv_a0, conv_a_w, conv_a_b)
    q, k = jnp.split(qk, 2, axis=-1)
    qh = q.reshape(bsz, T, H_A, DK_A).astype(f32)
    kh = k.reshape(bsz, T, H_A, DK_A).astype(f32) * (DK_A ** -0.5)
    vh = v.reshape(bsz, T, H_A, DV_A).astype(f32)
    log_i = ig.astype(f32) + gate_b[0].astype(f32)
    log_f = jax.nn.log_sigmoid(fg.astype(f32) + gate_b[1].astype(f32))
    xs_a = tuple(to_chunks(a, chunk) for a in (qh, kh, vh, log_i, log_f))
    (C1, n1, m1), h_a = lax.scan(mlstm_chunk, (C0.astype(f32), n0.astype(f32), m0.astype(f32)), xs_a)
    h_a = head_layernorm(from_chunks(h_a), mlstm_norm_g)
    h_a = (h_a * jax.nn.sigmoid(o.astype(f32))).astype(x.dtype)

    xbc, conv_b1 = causal_conv_silu(xbc, conv_b0, conv_b_w, conv_b_b)
    xs_b, Bm, Cm = jnp.split(xbc, [D_B, D_B + G_B * N_STATE], axis=-1)
    xh = xs_b.reshape(bsz, T, H_B, P_B).astype(f32)
    Bm = Bm.reshape(bsz, T, G_B, N_STATE).astype(f32)
    Cm = Cm.reshape(bsz, T, G_B, N_STATE).astype(f32)
    dt = jax.nn.softplus(dt.astype(f32) + dt_bias.astype(f32))
    A = -jnp.exp(a_log.astype(f32))
    xs_s = tuple(to_chunks(a, chunk) for a in (xh, dt, Bm, Cm))
    S1, y_b = lax.scan(functools.partial(ssd_chunk, A), S0.astype(f32), xs_s)
    y_b = from_chunks(y_b) + d_skip.astype(f32)[:, None] * xh
    y_b = y_b.reshape(bsz, T, D_B) * jax.nn.silu(z.astype(f32))
    y_b = rmsnorm(y_b, ssm_norm_g).astype(x.dtype)

    mixed = jnp.einsum('bte,ed->btd', jnp.concatenate([h_a, y_b], axis=-1), w_out)
    x = x + g1 * mixed
    h2 = rmsnorm(x, norm2_g) * (1 + sc2) + sh2
    x = peer_residual(x.reshape(bsz * T, -1), h2.reshape(bsz * T, -1), g2.reshape(bsz, -1), T,
                      *peer_w).reshape(bsz, T, -1)
    return x, (C1, n1, m1, conv_a1, S1, conv_b1)


def _final_norm_kernel(x_ref, g_ref, o_ref):
    x = x_ref[...]
    y = x * lax.rsqrt(jnp.mean(x * x, axis=-1, keepdims=True) + EPS)
    o_ref[...] = y * g_ref[...]


def _final_norm(x, g):
    shp = x.shape
    x2 = x.reshape(-1, shp[-1])
    n = x2.shape[0]
    tm = min(n, 256)
    out = pl.pallas_call(
        _final_norm_kernel,
        grid=(n // tm,),
        in_specs=[pl.BlockSpec((tm, shp[-1]), lambda i: (i, 0)),
                  pl.BlockSpec((1, shp[-1]), lambda i: (0, 0))],
        out_specs=pl.BlockSpec((tm, shp[-1]), lambda i: (i, 0)),
        out_shape=jax.ShapeDtypeStruct(x2.shape, x2.dtype),
    )(x2, g.reshape(1, -1))
    return out.reshape(shp)


def kernel(x_prompt, x_sample, c_prompt, c_sample, state_mlstm_c, state_mlstm_n, state_mlstm_m, state_mlstm_conv, state_ssm, state_ssm_conv, w_ada, b_ada, norm1_g, norm2_g, w_in, mlstm_gate_b, conv_a_w, conv_a_b, mlstm_norm_g, conv_b_w, conv_b_b, dt_bias, a_log, d_skip, ssm_norm_g, w_out, peer_w_query, peer_sub_keys, peer_u, peer_v, final_norm_g):
    f32, bf16 = jnp.float32, jnp.bfloat16
    bp = x_prompt.shape[0]
    zero_state = (jnp.zeros((bp, H_A, DK_A, DV_A), f32),
                  jnp.zeros((bp, H_A, DK_A), f32),
                  jnp.zeros((bp, H_A), f32),
                  jnp.zeros((bp, CONV_W - 1, D_CONV_A), x_prompt.dtype),
                  jnp.zeros((bp, H_B, P_B, N_STATE), f32),
                  jnp.zeros((bp, CONV_W - 1, D_CONV_B), x_prompt.dtype))
    hp, hs = x_prompt, x_sample
    new_p, new_s = [], []
    for l in range(DEPTH):
        wl = (w_ada[l], b_ada[l], norm1_g[l], norm2_g[l], w_in[l], mlstm_gate_b[l], conv_a_w[l],
              conv_a_b[l], mlstm_norm_g[l], conv_b_w[l], conv_b_b[l], dt_bias[l], a_log[l],
              d_skip[l], ssm_norm_g[l], w_out[l],
              (peer_w_query[l].T.astype(bf16), peer_sub_keys[l].astype(bf16),
               peer_u[l].astype(bf16), peer_v[l].T.astype(bf16)))
        hp, st_p = hybrid_layer(hp, c_prompt, zero_state, wl)
        st_in = (state_mlstm_c[l], state_mlstm_n[l], state_mlstm_m[l], state_mlstm_conv[l],
                 state_ssm[l], state_ssm_conv[l])
        hs, st_s = hybrid_layer(hs, c_sample, st_in, wl)
        new_p.append(st_p)
        new_s.append(st_s)
    y_prompt = _final_norm(hp, final_norm_g)
    y_sample = _final_norm(hs, final_norm_g)
    dtypes = (state_mlstm_c.dtype, state_mlstm_n.dtype, state_mlstm_m.dtype,
              state_mlstm_conv.dtype, state_ssm.dtype, state_ssm_conv.dtype)

    def stacked(states, i):
        return jnp.stack([s[i] for s in states], axis=0).astype(dtypes[i])

    return (y_prompt, y_sample,
            stacked(new_p, 0), stacked(new_p, 1), stacked(new_p, 2),
            stacked(new_p, 3), stacked(new_p, 4), stacked(new_p, 5),
            stacked(new_s, 0), stacked(new_s, 1), stacked(new_s, 2),
            stacked(new_s, 3), stacked(new_s, 4), stacked(new_s, 5))
```

```python
import math, functools
import jax, jax.numpy as jnp
from jax import lax
import numpy as np
from jax.experimental import pallas as pl
from jax.experimental.pallas import tpu as pltpu

D_MODEL = 1024
BATCH = 8
SEQ = 2048
DEPTH = 2
DEC_BATCH = 128
DEC_SEQ = 1

D_A = D_MODEL
H_A = 8
DK_A = D_A // H_A
DV_A = D_A // H_A
D_B = D_MODEL
P_B = 64
H_B = D_B // P_B
N_STATE = 128
G_B = 2
CONV_W = 4
D_CONV_A = 2 * D_A
D_CONV_B = D_B + 2 * G_B * N_STATE
D_MIX = D_A + D_B
SPLIT_SIZES = (D_A, D_A, D_A, D_A, H_A, H_A, D_B, D_CONV_B, H_B)
D_PROJ = sum(SPLIT_SIZES)
CHUNK = 128
PEER_HEADS = 8
N_KEYS = 128
N_EXPERTS = N_KEYS * N_KEYS
D_KEY = 256
PEER_TOPK = 16
PEER_BLOCK = 256
EPS = 1e-6


def rmsnorm(x, g):
    xf = x.astype(jnp.float32)
    y = xf * lax.rsqrt(jnp.mean(xf * xf, axis=-1, keepdims=True) + EPS)
    return (y * g.astype(jnp.float32)).astype(x.dtype)


def head_layernorm(h, g):
    hc = h - jnp.mean(h, axis=-1, keepdims=True)
    y = hc * lax.rsqrt(jnp.mean(hc * hc, axis=-1, keepdims=True) + EPS)
    return y.reshape(h.shape[0], h.shape[1], -1) * g.astype(jnp.float32)


def causal_conv_silu(x, buf, w, b):
    T = x.shape[1]
    xp = jnp.concatenate([buf.astype(x.dtype), x], axis=1)
    y = b + w[0] * xp[:, 0:T]
    for j in range(1, CONV_W):
        y = y + w[j] * xp[:, j:j + T]
    return jax.nn.silu(y), xp[:, xp.shape[1] - (CONV_W - 1):]


def to_chunks(a, L):
    b, T = a.shape[0], a.shape[1]
    return jnp.moveaxis(a.reshape((b, T // L, L) + a.shape[2:]), 1, 0)


def from_chunks(a):
    a = jnp.moveaxis(a, 0, 1)
    return a.reshape((a.shape[0], a.shape[1] * a.shape[2]) + a.shape[3:])


def mlstm_chunk(carry, inp):
    C, n, m = carry
    q, k, v, ig, lf = inp
    L = q.shape[1]
    bt = jnp.moveaxis(jnp.cumsum(lf, axis=1), 1, 2)
    igt = jnp.moveaxis(ig, 1, 2)
    causal = jnp.tril(jnp.ones((L, L), dtype=bool))
    log_d = jnp.where(causal, bt[..., :, None] - bt[..., None, :] + igt[..., None, :], -jnp.inf)
    log_inter = bt + m[..., None]
    m_t = jnp.maximum(log_inter, jnp.max(log_d, axis=-1))
    dmat = jnp.exp(log_d - m_t[..., None])
    inter = jnp.exp(log_inter - m_t)
    s = jnp.einsum('blhd,bshd->bhls', q, k) * dmat
    inter_l = jnp.moveaxis(inter, 1, 2)[..., None]
    num = jnp.einsum('bhls,bshv->blhv', s, v) + jnp.einsum('blhd,bhdv->blhv', q, C) * inter_l
    den = jnp.sum(s, axis=-1) + jnp.einsum('blhd,bhd->bhl', q, n) * inter
    denom = jnp.maximum(jnp.abs(den), jnp.exp(-m_t))
    h = num / jnp.moveaxis(denom, 1, 2)[..., None]
    b_last = bt[..., -1]
    log_w = b_last[..., None] - bt + igt
    m_new = jnp.maximum(b_last + m, jnp.max(log_w, axis=-1))
    w = jnp.exp(log_w - m_new[..., None])
    decay = jnp.exp(b_last + m - m_new)
    C_new = decay[..., None, None] * C + jnp.einsum('bhs,bshd,bshv->bhdv', w, k, v)
    n_new = decay[..., None] * n + jnp.einsum('bhs,bshd->bhd', w, k)
    return (C_new, n_new, m_new), h


def ssd_chunk(A, S, inp):
    x, dt, Bm, Cm = inp
    L = x.shape[1]
    rep = H_B // G_B
    Bh = jnp.repeat(Bm, rep, axis=2)
    Ch = jnp.repeat(Cm, rep, axis=2)
    cum = jnp.cumsum(jnp.moveaxis(dt * A, 1, 2), axis=-1)
    causal = jnp.tril(jnp.ones((L, L), dtype=bool))
    decay = jnp.exp(jnp.where(causal, cum[..., :, None] - cum[..., None, :], -jnp.inf))
    xdt = x * dt[..., None]
    scores = jnp.einsum('blhn,bshn->bhls', Ch, Bh) * decay
    y = jnp.einsum('bhls,bshp->blhp', scores, xdt) \
        + jnp.einsum('blhn,bhpn->blhp', Ch, S) * jnp.moveaxis(jnp.exp(cum), 1, 2)[..., None]
    w = jnp.exp(cum[..., -1:] - cum)
    S_new = jnp.exp(cum[..., -1])[..., None, None] * S + jnp.einsum('bhs,bshp,bshn->bhpn', w, xdt, Bh)
    return S_new, y


SUB = 128
STRIP = 256
NEG_INF = float('-inf')


def _topk_rank(s, vals_ref):
    iota = lax.broadcasted_iota(jnp.int32, s.shape, 0)
    rank = jnp.full(s.shape, float(PEER_TOPK), jnp.float32)
    work = s
    for i in range(PEER_TOPK):
        m = jnp.max(work, axis=0, keepdims=True)
        idx = jnp.min(jnp.where(work == m, iota, N_KEYS), axis=0, keepdims=True)
        first = iota == idx
        rank = jnp.where(first, float(i), rank)
        work = jnp.where(first, NEG_INF, work)
        vals_ref[i:i + 1, :] = m
    return rank


def _staircase(v1_ref, v2_ref):
    v1 = v1_ref[...]
    a16 = lax.broadcasted_iota(jnp.int32, v1.shape, 0).astype(jnp.float32) * float(PEER_TOPK)
    L = jnp.zeros(v1.shape, jnp.float32)
    for _ in range(PEER_TOPK):
        v2sel = jnp.full(v1.shape, NEG_INF, jnp.float32)
        for b in range(PEER_TOPK):
            v2sel = jnp.where(L == float(b), v2_ref[b:b + 1, :], v2sel)
        f = v1 + v2sel
        m = jnp.max(f, axis=0, keepdims=True)
        pos = a16 + L
        eq = f == m
        p = jnp.min(jnp.where(eq, pos, 1e9), axis=0, keepdims=True)
        L = L + jnp.where(eq & (pos == p), 1.0, 0.0)
    return L


def _peer_select_kernel(h2t_ref, wqt_ref, keys_ref, r2_ref, lim_ref, e1_ref, e2_ref,
                        q_scr, v1_scr, v2_scr, l_scr):
    tT = h2t_ref.shape[1]
    qt = jnp.dot(wqt_ref[...], h2t_ref[...], preferred_element_type=jnp.float32)
    q_scr[...] = qt.astype(q_scr.dtype)

    def head_body(h, carry):
        base = pl.multiple_of(h * D_KEY, D_KEY)
        s1 = jnp.dot(keys_ref[0], q_scr[pl.ds(base, N_KEYS), :],
                     preferred_element_type=jnp.float32)
        s2 = jnp.dot(keys_ref[1], q_scr[pl.ds(base + N_KEYS, N_KEYS), :],
                     preferred_element_type=jnp.float32)
        for sub in range(tT // SUB):
            sl = slice(sub * SUB, (sub + 1) * SUB)
            s1b, s2b = s1[:, sl], s2[:, sl]
            rank1 = _topk_rank(s1b, v1_scr)
            rank2 = _topk_rank(s2b, v2_scr)
            L = _staircase(v1_scr, v2_scr)
            l_scr[...] = L
            ev1 = jnp.exp(v1_scr[...] - v1_scr[0:1, :])
            csum = jnp.zeros(L.shape, jnp.float32)
            for b in range(PEER_TOPK):
                eb = jnp.exp(v2_scr[b:b + 1, :] - v2_scr[0:1, :])
                csum = csum + jnp.where(L > float(b), eb, 0.0)
            z = jnp.sum(ev1 * csum, axis=0, keepdims=True)
            lim = jnp.zeros(rank1.shape, jnp.float32)
            for a in range(PEER_TOPK):
                lim = jnp.where(rank1 == float(a), l_scr[a:a + 1, :], lim)
            r2_ref[h, :, sl] = rank2.astype(r2_ref.dtype)
            lim_ref[h, :, sl] = lim
            e1_ref[h, :, sl] = jnp.exp(s1b - v1_scr[0:1, :]) / z
            e2_ref[h, :, sl] = jnp.exp(s2b - v2_scr[0:1, :]).astype(e2_ref.dtype)
        return carry

    lax.fori_loop(0, PEER_HEADS, head_body, 0)


def _peer_dense_kernel(h2t_ref, r2_ref, lim_ref, e1_ref, e2_ref, u_ref, vt_ref, x_ref, g2_ref,
                       out_ref, acc_ref, at0_scr, at1_scr, w_scr):
    i, j = pl.program_id(0), pl.program_id(1)
    n_e = pl.num_programs(1) - 1
    tE, tT = at0_scr.shape
    n_e1 = tE // N_KEYS
    sel_dt = r2_ref.dtype

    @pl.when((i == 0) & (j == 0))
    def _():
        at1_scr[...] = jnp.zeros_like(at1_scr)

    @pl.when(j == 0)
    def _():
        acc_ref[...] = jnp.zeros_like(acc_ref)

    def step(at_new, at_old):
        at_new[...] = jnp.dot(u_ref[...], h2t_ref[...], preferred_element_type=jnp.float32)
        jb = jnp.maximum(j - 1, 0)
        live = jnp.where(j > 0, 1.0, 0.0)
        for c in range(n_e1):
            e1 = jb * n_e1 + c
            rows = slice(c * N_KEYS, (c + 1) * N_KEYS)
            lim_rows = [lim_ref[h, pl.ds(e1, 1), :] * live for h in range(PEER_HEADS)]
            e1_rows = [e1_ref[h, pl.ds(e1, 1), :] for h in range(PEER_HEADS)]
            lim_rows = [r.astype(sel_dt) for r in lim_rows]
            e1_rows = [r.astype(sel_dt) for r in e1_rows]
            strip = min(tT, STRIP)
            for s in range(tT // strip):
                cols = slice(s * strip, (s + 1) * strip)
                g = None
                for h in range(PEER_HEADS):
                    sel = r2_ref[h, :, cols] < lim_rows[h][:, cols]
                    t = jnp.where(sel, e2_ref[h, :, cols] * e1_rows[h][:, cols], jnp.zeros((), sel_dt))
                    g = t if g is None else g + t
                a = at_old[rows, cols]
                act = 0.5 * a * (1.0 + lax.erf(a * (2.0 ** -0.5)))
                w_scr[rows, cols] = (act.astype(sel_dt) * g).astype(w_scr.dtype)
        acc_ref[...] += jnp.dot(vt_ref[...], w_scr[...], preferred_element_type=jnp.float32)

    @pl.when(j % 2 == 0)
    def _():
        step(at0_scr, at1_scr)

    @pl.when(j % 2 == 1)
    def _():
        step(at1_scr, at0_scr)

    @pl.when(j == n_e)
    def _():
        out_ref[...] = x_ref[...] + g2_ref[...] * acc_ref[...].T


def _peer_block_sizes(ntok):
    t_sel = min(ntok, 256)
    t_dense = min(ntok, 512)
    return t_sel, t_dense, 512


def peer_residual(x, h2, g2, rows_per_g2, wqt, keys, u, vt):
    ntok, D = x.shape
    t_sel, t_dense, t_e = _peer_block_sizes(ntok)
    mxu_dt = u.dtype
    h2t = h2.T.astype(mxu_dt)
    row_shape = jax.ShapeDtypeStruct((PEER_HEADS, N_KEYS, ntok), jnp.float32)
    col_shape = jax.ShapeDtypeStruct((PEER_HEADS, N_KEYS, ntok), mxu_dt)
    sel_spec = pl.BlockSpec((PEER_HEADS, N_KEYS, t_sel), lambda i: (0, 0, i))
    r2, lim, e1, e2 = pl.pallas_call(
        _peer_select_kernel,
        grid=(ntok // t_sel,),
        in_specs=[pl.BlockSpec((D, t_sel), lambda i: (0, i)),
                  pl.BlockSpec(wqt.shape, lambda i: (0, 0)),
                  pl.BlockSpec(keys.shape, lambda i: (0, 0, 0))],
        out_specs=[sel_spec] * 4,
        out_shape=[col_shape, row_shape, row_shape, col_shape],
        scratch_shapes=[pltpu.VMEM((PEER_HEADS * D_KEY, t_sel), mxu_dt),
                        pltpu.VMEM((PEER_TOPK, SUB), jnp.float32),
                        pltpu.VMEM((PEER_TOPK, SUB), jnp.float32),
                        pltpu.VMEM((PEER_TOPK, SUB), jnp.float32)],
        compiler_params=pltpu.CompilerParams(dimension_semantics=("arbitrary",)),
        name="peer_select",
    )(h2t, wqt, keys)

    if rows_per_g2 == 1:
        g2_spec = pl.BlockSpec((t_dense, D), lambda i, j: (i, 0))
    else:
        assert rows_per_g2 % t_dense == 0
        g2 = g2.reshape(-1, 1, D)
        g2_spec = pl.BlockSpec((None, 1, D), lambda i, j: (i // (rows_per_g2 // t_dense), 0, 0))
    n_e = N_EXPERTS // t_e
    tok_spec = pl.BlockSpec((PEER_HEADS, N_KEYS, t_dense), lambda i, j: (0, 0, i))
    return pl.pallas_call(
        _peer_dense_kernel,
        grid=(ntok // t_dense, n_e + 1),
        in_specs=[pl.BlockSpec((D, t_dense), lambda i, j: (0, i)),
                  tok_spec, tok_spec, tok_spec, tok_spec,
                  pl.BlockSpec((t_e, D), lambda i, j: (jnp.minimum(j, n_e - 1), 0)),
                  pl.BlockSpec((D, t_e), lambda i, j: (0, jnp.maximum(j - 1, 0))),
                  pl.BlockSpec((t_dense, D), lambda i, j: (i, 0)),
                  g2_spec],
        out_specs=pl.BlockSpec((t_dense, D), lambda i, j: (i, 0)),
        out_shape=jax.ShapeDtypeStruct((ntok, D), jnp.float32),
        scratch_shapes=[pltpu.VMEM((D, t_dense), jnp.float32),
                        pltpu.VMEM((t_e, t_dense), jnp.float32),
                        pltpu.VMEM((t_e, t_dense), jnp.float32),
                        pltpu.VMEM((t_e, t_dense), mxu_dt)],
        compiler_params=pltpu.CompilerParams(
            dimension_semantics=("arbitrary", "arbitrary"),
            vmem_limit_bytes=48 * 1024 * 1024),
        name="peer_dense",
    )(h2t, r2, lim, e1, e2, u, vt, x, g2)


def hybrid_layer(x, c, st, wl):
    (w_ada, b_ada, norm1_g, norm2_g, w_in, gate_b, conv_a_w, conv_a_b, mlstm_norm_g,
     conv_b_w, conv_b_b, dt_bias, a_log, d_skip, ssm_norm_g, w_out, peer_w) = wl
    C0, n0, m0, conv_a0, S0, conv_b0 = st
    f32 = jnp.float32
    bsz, T, _ = x.shape
    chunk = CHUNK if T % CHUNK == 0 else T
    mod = jnp.einsum('bd,de->be', jax.nn.silu(c), w_ada) + b_ada
    sh1, sc1, g1, sh2, sc2, g2 = jnp.split(mod[:, None, :], 6, axis=-1)
    h = rmsnorm(x, norm1_g) * (1 + sc1) + sh1
    proj = jnp.einsum('btd,de->bte', h, w_in)
    bounds = np.cumsum(SPLIT_SIZES)[:-1].tolist()
    q, k, v, o, ig, fg, z, xbc, dt = jnp.split(proj, bounds, axis=-1)

    qk, conv_a1 = causal_conv_silu(jnp.concatenate([q, k], axis=-1), conv_a0, conv_a_w, conv_a_b)
    q, k = jnp.split(qk, 2, axis=-1)
    qh = q.reshape(bsz, T, H_A, DK_A).astype(f32)
    kh = k.reshape(bsz, T, H_A, DK_A).astype(f32) * (DK_A ** -0.5)
    vh = v.reshape(bsz, T, H_A, DV_A).astype(f32)
    log_i = ig.astype(f32) + gate_b[0].astype(f32)
    log_f = jax.nn.log_sigmoid(fg.astype(f32) + gate_b[1].astype(f32))
    xs_a = tuple(to_chunks(a, chunk) for a in (qh, kh, vh, log_i, log_f))
    (C1, n1, m1), h_a = lax.scan(mlstm_chunk, (C0.astype(f32), n0.astype(f32), m0.astype(f32)), xs_a)
    h_a = head_layernorm(from_chunks(h_a), mlstm_norm_g)
    h_a = (h_a * jax.nn.sigmoid(o.astype(f32))).astype(x.dtype)

    xbc, conv_b1 = causal_conv_silu(xbc, conv_b0, conv_b_w, conv_b_b)
    xs_b, Bm, Cm = jnp.split(xbc, [D_B, D_B + G_B * N_STATE], axis=-1)
    xh = xs_b.reshape(bsz, T, H_B, P_B).astype(f32)
    Bm = Bm.reshape(bsz, T, G_B, N_STATE).astype(f32)
    Cm = Cm.reshape(bsz, T, G_B, N_STATE).astype(f32)
    dt = jax.nn.softplus(dt.astype(f32) + dt_bias.astype(f32))
    A = -jnp.exp(a_log.astype(f32))
    xs_s = tuple(to_chunks(a, chunk) for a in (xh, dt, Bm, Cm))
    S1, y_b = lax.scan(functools.partial(ssd_chunk, A), S0.astype(f32), xs_s)
    y_b = from_chunks(y_b) + d_skip.astype(f32)[:, None] * xh
    y_b = y_b.reshape(bsz, T, D_B) * jax.nn.silu(z.astype(f32))
    y_b = rmsnorm(y_b, ssm_norm_g).astype(x.dtype)

    mixed = jnp.einsum('bte,ed->btd', jnp.concatenate([h_a, y_b], axis=-1), w_out)
    x = x + g1 * mixed
    h2 = rmsnorm(x, norm2_g) * (1 + sc2) + sh2
    x = peer_residual(x.reshape(bsz * T, -1), h2.reshape(bsz * T, -1), g2.reshape(bsz, -1), T,
                      *peer_w).reshape(bsz, T, -1)
    return x, (C1, n1, m1, conv_a1, S1, conv_b1)


def _final_norm_kernel(x_ref, g_ref, o_ref):
    x = x_ref[...]
    y = x * lax.rsqrt(jnp.mean(x * x, axis=-1, keepdims=True) + EPS)
    o_ref[...] = y * g_ref[...]


def _final_norm(x, g):
    shp = x.shape
    x2 = x.reshape(-1, shp[-1])
    n = x2.shape[0]
    tm = min(n, 256)
    out = pl.pallas_call(
        _final_norm_kernel,
        grid=(n // tm,),
        in_specs=[pl.BlockSpec((tm, shp[-1]), lambda i: (i, 0)),
                  pl.BlockSpec((1, shp[-1]), lambda i: (0, 0))],
        out_specs=pl.BlockSpec((tm, shp[-1]), lambda i: (i, 0)),
        out_shape=jax.ShapeDtypeStruct(x2.shape, x2.dtype),
    )(x2, g.reshape(1, -1))
    return out.reshape(shp)


def kernel(x_prompt, x_sample, c_prompt, c_sample, state_mlstm_c, state_mlstm_n, state_mlstm_m, state_mlstm_conv, state_ssm, state_ssm_conv, w_ada, b_ada, norm1_g, norm2_g, w_in, mlstm_gate_b, conv_a_w, conv_a_b, mlstm_norm_g, conv_b_w, conv_b_b, dt_bias, a_log, d_skip, ssm_norm_g, w_out, peer_w_query, peer_sub_keys, peer_u, peer_v, final_norm_g):
    f32, bf16 = jnp.float32, jnp.bfloat16
    bp = x_prompt.shape[0]
    zero_state = (jnp.zeros((bp, H_A, DK_A, DV_A), f32),
                  jnp.zeros((bp, H_A, DK_A), f32),
                  jnp.zeros((bp, H_A), f32),
                  jnp.zeros((bp, CONV_W - 1, D_CONV_A), x_prompt.dtype),
                  jnp.zeros((bp, H_B, P_B, N_STATE), f32),
                  jnp.zeros((bp, CONV_W - 1, D_CONV_B), x_prompt.dtype))
    hp, hs = x_prompt, x_sample
    new_p, new_s = [], []
    for l in range(DEPTH):
        wl = (w_ada[l], b_ada[l], norm1_g[l], norm2_g[l], w_in[l], mlstm_gate_b[l], conv_a_w[l],
              conv_a_b[l], mlstm_norm_g[l], conv_b_w[l], conv_b_b[l], dt_bias[l], a_log[l],
              d_skip[l], ssm_norm_g[l], w_out[l],
              (peer_w_query[l].T.astype(bf16), peer_sub_keys[l].astype(bf16),
               peer_u[l].astype(bf16), peer_v[l].T.astype(bf16)))
        hp, st_p = hybrid_layer(hp, c_prompt, zero_state, wl)
        st_in = (state_mlstm_c[l], state_mlstm_n[l], state_mlstm_m[l], state_mlstm_conv[l],
                 state_ssm[l], state_ssm_conv[l])
        hs, st_s = hybrid_layer(hs, c_sample, st_in, wl)
        new_p.append(st_p)
        new_s.append(st_s)
    y_prompt = _final_norm(hp, final_norm_g)
    y_sample = _final_norm(hs, final_norm_g)
    dtypes = (state_mlstm_c.dtype, state_mlstm_n.dtype, state_mlstm_m.dtype,
              state_mlstm_conv.dtype, state_ssm.dtype, state_ssm_conv.dtype)

    def stacked(states, i):
        return jnp.stack([s[i] for s in states], axis=0).astype(dtypes[i])

    return (y_prompt, y_sample,
            stacked(new_p, 0), stacked(new_p, 1), stacked(new_p, 2),
            stacked(new_p, 3), stacked(new_p, 4), stacked(new_p, 5),
            stacked(new_s, 0), stacked(new_s, 1), stacked(new_s, 2),
            stacked(new_s, 3), stacked(new_s, 4), stacked(new_s, 5))
```

```python
import math, functools
import jax, jax.numpy as jnp
from jax import lax
import numpy as np
from jax.experimental import pallas as pl
from jax.experimental.pallas import tpu as pltpu

D_MODEL = 1024
BATCH = 8
SEQ = 2048
DEPTH = 2
DEC_BATCH = 128
DEC_SEQ = 1

D_A = D_MODEL
H_A = 8
DK_A = D_A // H_A
DV_A = D_A // H_A
D_B = D_MODEL
P_B = 64
H_B = D_B // P_B
N_STATE = 128
G_B = 2
CONV_W = 4
D_CONV_A = 2 * D_A
D_CONV_B = D_B + 2 * G_B * N_STATE
D_MIX = D_A + D_B
SPLIT_SIZES = (D_A, D_A, D_A, D_A, H_A, H_A, D_B, D_CONV_B, H_B)
D_PROJ = sum(SPLIT_SIZES)
CHUNK = 128
PEER_HEADS = 8
N_KEYS = 128
N_EXPERTS = N_KEYS * N_KEYS
D_KEY = 256
PEER_TOPK = 16
PEER_BLOCK = 256
EPS = 1e-6


def rmsnorm(x, g):
    xf = x.astype(jnp.float32)
    y = xf * lax.rsqrt(jnp.mean(xf * xf, axis=-1, keepdims=True) + EPS)
    return (y * g.astype(jnp.float32)).astype(x.dtype)


def head_layernorm(h, g):
    hc = h - jnp.mean(h, axis=-1, keepdims=True)
    y = hc * lax.rsqrt(jnp.mean(hc * hc, axis=-1, keepdims=True) + EPS)
    return y.reshape(h.shape[0], h.shape[1], -1) * g.astype(jnp.float32)


def causal_conv_silu(x, buf, w, b):
    T = x.shape[1]
    xp = jnp.concatenate([buf.astype(x.dtype), x], axis=1)
    y = b + w[0] * xp[:, 0:T]
    for j in range(1, CONV_W):
        y = y + w[j] * xp[:, j:j + T]
    return jax.nn.silu(y), xp[:, xp.shape[1] - (CONV_W - 1):]


def to_chunks(a, L):
    b, T = a.shape[0], a.shape[1]
    return jnp.moveaxis(a.reshape((b, T // L, L) + a.shape[2:]), 1, 0)


def from_chunks(a):
    a = jnp.moveaxis(a, 0, 1)
    return a.reshape((a.shape[0], a.shape[1] * a.shape[2]) + a.shape[3:])


def mlstm_chunk(carry, inp):
    C, n, m = carry
    q, k, v, ig, lf = inp
    L = q.shape[1]
    bt = jnp.moveaxis(jnp.cumsum(lf, axis=1), 1, 2)
    igt = jnp.moveaxis(ig, 1, 2)
    causal = jnp.tril(jnp.ones((L, L), dtype=bool))
    log_d = jnp.where(causal, bt[..., :, None] - bt[..., None, :] + igt[..., None, :], -jnp.inf)
    log_inter = bt + m[..., None]
    m_t = jnp.maximum(log_inter, jnp.max(log_d, axis=-1))
    dmat = jnp.exp(log_d - m_t[..., None])
    inter = jnp.exp(log_inter - m_t)
    s = jnp.einsum('blhd,bshd->bhls', q, k) * dmat
    inter_l = jnp.moveaxis(inter, 1, 2)[..., None]
    num = jnp.einsum('bhls,bshv->blhv', s, v) + jnp.einsum('blhd,bhdv->blhv', q, C) * inter_l
    den = jnp.sum(s, axis=-1) + jnp.einsum('blhd,bhd->bhl', q, n) * inter
    denom = jnp.maximum(jnp.abs(den), jnp.exp(-m_t))
    h = num / jnp.moveaxis(denom, 1, 2)[..., None]
    b_last = bt[..., -1]
    log_w = b_last[..., None] - bt + igt
    m_new = jnp.maximum(b_last + m, jnp.max(log_w, axis=-1))
    w = jnp.exp(log_w - m_new[..., None])
    decay = jnp.exp(b_last + m - m_new)
    C_new = decay[..., None, None] * C + jnp.einsum('bhs,bshd,bshv->bhdv', w, k, v)
    n_new = decay[..., None] * n + jnp.einsum('bhs,bshd->bhd', w, k)
    return (C_new, n_new, m_new), h


def ssd_chunk(A, S, inp):
    x, dt, Bm, Cm = inp
    L = x.shape[1]
    rep = H_B // G_B
    Bh = jnp.repeat(Bm, rep, axis=2)
    Ch = jnp.repeat(Cm, rep, axis=2)
    cum = jnp.cumsum(jnp.moveaxis(dt * A, 1, 2), axis=-1)
    causal = jnp.tril(jnp.ones((L, L), dtype=bool))
    decay = jnp.exp(jnp.where(causal, cum[..., :, None] - cum[..., None, :], -jnp.inf))
    xdt = x * dt[..., None]
    scores = jnp.einsum('blhn,bshn->bhls', Ch, Bh) * decay
    y = jnp.einsum('bhls,bshp->blhp', scores, xdt) \
        + jnp.einsum('blhn,bhpn->blhp', Ch, S) * jnp.moveaxis(jnp.exp(cum), 1, 2)[..., None]
    w = jnp.exp(cum[..., -1:] - cum)
    S_new = jnp.exp(cum[..., -1])[..., None, None] * S + jnp.einsum('bhs,bshp,bshn->bhpn', w, xdt, Bh)
    return S_new, y


SUB = 128
STRIP = 256
NEG_INF = float('-inf')


def _topk_rank(s, vals_ref):
    iota = lax.broadcasted_iota(jnp.int32, s.shape, 0)
    rank = jnp.full(s.shape, float(PEER_TOPK), jnp.float32)
    work = s
    for i in range(PEER_TOPK):
        m = jnp.max(work, axis=0, keepdims=True)
        idx = jnp.min(jnp.where(work == m, iota, N_KEYS), axis=0, keepdims=True)
        first = iota == idx
        rank = jnp.where(first, float(i), rank)
        work = jnp.where(first, NEG_INF, work)
        vals_ref[i:i + 1, :] = m
    return rank


def _staircase(v1_ref, v2_ref):
    v1 = v1_ref[...]
    a16 = lax.broadcasted_iota(jnp.int32, v1.shape, 0).astype(jnp.float32) * float(PEER_TOPK)
    L = jnp.zeros(v1.shape, jnp.float32)
    for _ in range(PEER_TOPK):
        v2sel = jnp.full(v1.shape, NEG_INF, jnp.float32)
        for b in range(PEER_TOPK):
            v2sel = jnp.where(L == float(b), v2_ref[b:b + 1, :], v2sel)
        f = v1 + v2sel
        m = jnp.max(f, axis=0, keepdims=True)
        pos = a16 + L
        eq = f == m
        p = jnp.min(jnp.where(eq, pos, 1e9), axis=0, keepdims=True)
        L = L + jnp.where(eq & (pos == p), 1.0, 0.0)
    return L


def _peer_select_kernel(h2t_ref, wqt_ref, keys_ref, r2_ref, lim_ref, e1_ref, e2_ref,
                        q_scr, v1_scr, v2_scr, l_scr):
    tT = h2t_ref.shape[1]
    qt = jnp.dot(wqt_ref[...], h2t_ref[...], preferred_element_type=jnp.float32)
    q_scr[...] = qt.astype(q_scr.dtype)

    def head_body(h, carry):
        base = pl.multiple_of(h * D_KEY, D_KEY)
        s1 = jnp.dot(keys_ref[0], q_scr[pl.ds(base, N_KEYS), :],
                     preferred_element_type=jnp.float32)
        s2 = jnp.dot(keys_ref[1], q_scr[pl.ds(base + N_KEYS, N_KEYS), :],
                     preferred_element_type=jnp.float32)
        for sub in range(tT // SUB):
            sl = slice(sub * SUB, (sub + 1) * SUB)
            s1b, s2b = s1[:, sl], s2[:, sl]
            rank1 = _topk_rank(s1b, v1_scr)
            rank2 = _topk_rank(s2b, v2_scr)
            L = _staircase(v1_scr, v2_scr)
            l_scr[...] = L
            ev1 = jnp.exp(v1_scr[...] - v1_scr[0:1, :])
            csum = jnp.zeros(L.shape, jnp.float32)
            for b in range(PEER_TOPK):
                eb = jnp.exp(v2_scr[b:b + 1, :] - v2_scr[0:1, :])
                csum = csum + jnp.where(L > float(b), eb, 0.0)
            z = jnp.sum(ev1 * csum, axis=0, keepdims=True)
            lim = jnp.zeros(rank1.shape, jnp.float32)
            for a in range(PEER_TOPK):
                lim = jnp.where(rank1 == float(a), l_scr[a:a + 1, :], lim)
            r2_ref[h, :, sl] = rank2.astype(r2_ref.dtype)
            lim_ref[h, :, sl] = lim
            e1_ref[h, :, sl] = jnp.exp(s1b - v1_scr[0:1, :]) / z
            e2_ref[h, :, sl] = jnp.exp(s2b - v2_scr[0:1, :]).astype(e2_ref.dtype)
        return carry

    lax.fori_loop(0, PEER_HEADS, head_body, 0)


def _peer_dense_kernel(h2t_ref, r2_ref, lim_ref, e1_ref, e2_ref, u_ref, vt_ref, x_ref, g2_ref,
                       out_ref, acc_ref, at0_scr, at1_scr, w_scr):
    i, j = pl.program_id(0), pl.program_id(1)
    n_e = pl.num_programs(1) - 1
    tE, tT = at0_scr.shape
    n_e1 = tE // N_KEYS
    sel_dt = r2_ref.dtype

    @pl.when((i == 0) & (j == 0))
    def _():
        at1_scr[...] = jnp.zeros_like(at1_scr)

    @pl.when(j == 0)
    def _():
        acc_ref[...] = jnp.zeros_like(acc_ref)

    def step(at_new, at_old):
        at_new[...] = jnp.dot(u_ref[...], h2t_ref[...], preferred_element_type=jnp.float32)
        jb = jnp.maximum(j - 1, 0)
        live = jnp.where(j > 0, 1.0, 0.0)
        for c in range(n_e1):
            e1 = jb * n_e1 + c
            rows = slice(c * N_KEYS, (c + 1) * N_KEYS)
            lim_rows = [lim_ref[h, pl.ds(e1, 1), :] * live for h in range(PEER_HEADS)]
            e1_rows = [e1_ref[h, pl.ds(e1, 1), :] for h in range(PEER_HEADS)]
            lim_rows = [r.astype(sel_dt) for r in lim_rows]
            e1_rows = [r.astype(sel_dt) for r in e1_rows]
            strip = min(tT, STRIP)
            for s in range(tT // strip):
                cols = slice(s * strip, (s + 1) * strip)
                g = None
                for h in range(PEER_HEADS):
                    sel = r2_ref[h, :, cols] < lim_rows[h][:, cols]
                    t = jnp.where(sel, e2_ref[h, :, cols] * e1_rows[h][:, cols], jnp.zeros((), sel_dt))
                    g = t if g is None else g + t
                a = at_old[rows, cols]
                act = 0.5 * a * (1.0 + lax.erf(a * (2.0 ** -0.5)))
                w_scr[rows, cols] = (act.astype(sel_dt) * g).astype(w_scr.dtype)
        acc_ref[...] += jnp.dot(vt_ref[...], w_scr[...], preferred_element_type=jnp.float32)

    @pl.when(j % 2 == 0)
    def _():
        step(at0_scr, at1_scr)

    @pl.when(j % 2 == 1)
    def _():
        step(at1_scr, at0_scr)

    @pl.when(j == n_e)
    def _():
        out_ref[...] = x_ref[...] + g2_ref[...] * acc_ref[...].T


PEER_EXPERT_TILE = 512


def _row_tile(ntok, rows_per_seq, cap):
    return min(ntok, cap) if rows_per_seq == 1 else min(rows_per_seq, cap)


def _peer_block_sizes(ntok, rows_per_seq):
    return min(ntok, 256), _row_tile(ntok, rows_per_seq, 512), PEER_EXPERT_TILE


def peer_residual(x, h2t, g2, rows_per_g2, wqt, keys, u, vt):
    ntok, D = x.shape
    t_sel, t_dense, t_e = _peer_block_sizes(ntok, rows_per_g2)
    assert vt.shape == (N_EXPERTS // t_e, D, t_e)
    mxu_dt = u.dtype
    row_shape = jax.ShapeDtypeStruct((PEER_HEADS, N_KEYS, ntok), jnp.float32)
    col_shape = jax.ShapeDtypeStruct((PEER_HEADS, N_KEYS, ntok), mxu_dt)
    sel_spec = pl.BlockSpec((PEER_HEADS, N_KEYS, t_sel), lambda i: (0, 0, i))
    r2, lim, e1, e2 = pl.pallas_call(
        _peer_select_kernel,
        grid=(ntok // t_sel,),
        in_specs=[pl.BlockSpec((D, t_sel), lambda i: (0, i)),
                  pl.BlockSpec(wqt.shape, lambda i: (0, 0)),
                  pl.BlockSpec(keys.shape, lambda i: (0, 0, 0))],
        out_specs=[sel_spec] * 4,
        out_shape=[col_shape, row_shape, row_shape, col_shape],
        scratch_shapes=[pltpu.VMEM((PEER_HEADS * D_KEY, t_sel), mxu_dt),
                        pltpu.VMEM((PEER_TOPK, SUB), jnp.float32),
                        pltpu.VMEM((PEER_TOPK, SUB), jnp.float32),
                        pltpu.VMEM((PEER_TOPK, SUB), jnp.float32)],
        compiler_params=pltpu.CompilerParams(dimension_semantics=("arbitrary",)),
        name="peer_select",
    )(h2t, wqt, keys)

    if rows_per_g2 == 1:
        g2_spec = pl.BlockSpec((t_dense, D), lambda i, j: (i, 0))
    else:
        assert rows_per_g2 % t_dense == 0
        g2 = g2.reshape(-1, 1, D)
        g2_spec = pl.BlockSpec((None, 1, D), lambda i, j: (i // (rows_per_g2 // t_dense), 0, 0))
    n_e = N_EXPERTS // t_e
    tok_spec = pl.BlockSpec((PEER_HEADS, N_KEYS, t_dense), lambda i, j: (0, 0, i))
    return pl.pallas_call(
        _peer_dense_kernel,
        grid=(ntok // t_dense, n_e + 1),
        in_specs=[pl.BlockSpec((D, t_dense), lambda i, j: (0, i)),
                  tok_spec, tok_spec, tok_spec, tok_spec,
                  pl.BlockSpec((t_e, D), lambda i, j: (jnp.minimum(j, n_e - 1), 0)),
                  pl.BlockSpec((None, D, t_e), lambda i, j: (jnp.maximum(j - 1, 0), 0, 0)),
                  pl.BlockSpec((t_dense, D), lambda i, j: (i, 0)),
                  g2_spec],
        out_specs=pl.BlockSpec((t_dense, D), lambda i, j: (i, 0)),
        out_shape=jax.ShapeDtypeStruct((ntok, D), jnp.float32),
        scratch_shapes=[pltpu.VMEM((D, t_dense), jnp.float32),
                        pltpu.VMEM((t_e, t_dense), jnp.float32),
                        pltpu.VMEM((t_e, t_dense), jnp.float32),
                        pltpu.VMEM((t_e, t_dense), mxu_dt)],
        compiler_params=pltpu.CompilerParams(
            dimension_semantics=("arbitrary", "arbitrary"),
            vmem_limit_bytes=48 * 1024 * 1024),
        name="peer_dense",
    )(h2t, r2, lim, e1, e2, u, vt, x, g2)


COL_QK, COL_V, COL_O, COL_Z, COL_XS = 0, 2, 3, 4, 5
COL_BC = (4 * D_A + 2 * D_B) // (2 * G_B * N_STATE)
N_BIG = 4 * D_A + D_B + D_CONV_B
GATE_W = 128


def _seq_rows(a, rows_per_seq, tm):
    D = a.shape[-1]
    if rows_per_seq == 1:
        return a, pl.BlockSpec((tm, D), lambda i, *_: (i, 0))
    assert rows_per_seq % tm == 0
    per = rows_per_seq // tm
    return a.reshape(-1, 1, D), pl.BlockSpec((None, 1, D), lambda i, *_: (i // per, 0, 0))


def _silu(x):
    return x / (1.0 + jnp.exp(-x))


def _adaln_kernel(c_ref, w_ref, b_ref, o_ref):
    a = _silu(c_ref[...]).astype(jnp.bfloat16)
    o_ref[...] = jnp.dot(a, w_ref[...].astype(jnp.bfloat16),
                         preferred_element_type=jnp.float32) + b_ref[...]


def adaln(c, w, b):
    n, D = c.shape
    tn = 512
    return pl.pallas_call(
        _adaln_kernel,
        grid=(w.shape[1] // tn,),
        in_specs=[pl.BlockSpec((n, D), lambda j: (0, 0)),
                  pl.BlockSpec((D, tn), lambda j: (0, j)),
                  pl.BlockSpec((1, tn), lambda j: (0, j))],
        out_specs=pl.BlockSpec((n, tn), lambda j: (0, j)),
        out_shape=jax.ShapeDtypeStruct((n, w.shape[1]), jnp.float32),
        compiler_params=pltpu.CompilerParams(dimension_semantics=("arbitrary",)),
        name="adaln",
    )(c, w, b.reshape(1, -1))


def _modulated_norm(x, g, sc, sh):
    y = x * lax.rsqrt(jnp.mean(x * x, axis=-1, keepdims=True) + EPS) * g
    return y * (1.0 + sc) + sh


def _inproj_kernel(x_ref, sc_ref, sh_ref, g_ref, wbig_ref, wgate_ref, big_ref, gate_ref, h_scr):
    @pl.when(pl.program_id(1) == 0)
    def _():
        h = _modulated_norm(x_ref[...], g_ref[...], sc_ref[...], sh_ref[...])
        h_scr[...] = h.astype(h_scr.dtype)
        gate_ref[...] = jnp.dot(h_scr[...], wgate_ref[...], preferred_element_type=jnp.float32)

    big_ref[...] = jnp.dot(h_scr[...], wbig_ref[...], preferred_element_type=jnp.float32)


def in_projection(x, sc, sh, rows_per_seq, g, w_big, w_gate):
    ntok, D = x.shape
    tm = _row_tile(ntok, rows_per_seq, 1024)
    tn = 512
    sc, sc_spec = _seq_rows(sc, rows_per_seq, tm)
    sh, sh_spec = _seq_rows(sh, rows_per_seq, tm)
    return pl.pallas_call(
        _inproj_kernel,
        grid=(ntok // tm, N_BIG // tn),
        in_specs=[pl.BlockSpec((tm, D), lambda i, j: (i, 0)), sc_spec, sh_spec,
                  pl.BlockSpec((1, D), lambda i, j: (0, 0)),
                  pl.BlockSpec((D, tn), lambda i, j: (0, j)),
                  pl.BlockSpec((D, 3 * GATE_W), lambda i, j: (0, 0))],
        out_specs=[pl.BlockSpec((tm, tn), lambda i, j: (i, j)),
                   pl.BlockSpec((tm, 3 * GATE_W), lambda i, j: (i, 0))],
        out_shape=[jax.ShapeDtypeStruct((ntok, N_BIG), jnp.float32),
                   jax.ShapeDtypeStruct((ntok, 3 * GATE_W), jnp.float32)],
        scratch_shapes=[pltpu.VMEM((tm, D), jnp.bfloat16)],
        compiler_params=pltpu.CompilerParams(
            dimension_semantics=("arbitrary", "arbitrary"), vmem_limit_bytes=48 * 1024 * 1024),
        name="in_projection",
    )(x, sc, sh, g.reshape(1, -1), w_big, w_gate)


def _outproj_kernel(ha_ref, yb_ref, w_ref, x_ref, g1_ref, sc_ref, sh_ref, g_ref, x1_ref, h2t_ref):
    d = ha_ref.shape[1]
    mixed = (jnp.dot(ha_ref[...], w_ref[0:d, :], preferred_element_type=jnp.float32)
             + jnp.dot(yb_ref[...], w_ref[d:, :], preferred_element_type=jnp.float32))
    x1 = x_ref[...] + g1_ref[...] * mixed
    x1_ref[...] = x1
    h2 = _modulated_norm(x1, g_ref[...], sc_ref[...], sh_ref[...])
    h2t_ref[...] = h2.T.astype(h2t_ref.dtype)


def out_projection(ha, yb, w_out, x, g1, sc, sh, rows_per_seq, g):
    ntok, D = x.shape
    tm = _row_tile(ntok, rows_per_seq, 512)
    g1, g1_spec = _seq_rows(g1, rows_per_seq, tm)
    sc, sc_spec = _seq_rows(sc, rows_per_seq, tm)
    sh, sh_spec = _seq_rows(sh, rows_per_seq, tm)
    row_spec = pl.BlockSpec((tm, D), lambda i: (i, 0))
    return pl.pallas_call(
        _outproj_kernel,
        grid=(ntok // tm,),
        in_specs=[row_spec, row_spec, pl.BlockSpec(w_out.shape, lambda i: (0, 0)), row_spec,
                  g1_spec, sc_spec, sh_spec, pl.BlockSpec((1, D), lambda i: (0, 0))],
        out_specs=[row_spec, pl.BlockSpec((D, tm), lambda i: (0, i))],
        out_shape=[jax.ShapeDtypeStruct((ntok, D), jnp.float32),
                   jax.ShapeDtypeStruct((D, ntok), jnp.bfloat16)],
        compiler_params=pltpu.CompilerParams(
            dimension_semantics=("arbitrary",), vmem_limit_bytes=48 * 1024 * 1024),
        name="out_projection",
    )(ha, yb, w_out, x, g1, sc, sh, g.reshape(1, -1))


def _conv_silu(buf_ref, x, w_ref, b_ref):
    L = x.shape[0]
    buf_ref[8:8 + L, :] = x
    y = b_ref[...] + w_ref[0:1, :] * buf_ref[5:5 + L, :]
    for j in range(1, CONV_W):
        y = y + w_ref[j:j + 1, :] * buf_ref[5 + j:5 + j + L, :]
    buf_ref[5:8, :] = buf_ref[5 + L:8 + L, :]
    return _silu(y)


def _split3(x):
    hi = x.astype(jnp.bfloat16)
    r = x - hi.astype(jnp.float32)
    mid = r.astype(jnp.bfloat16)
    lo = (r - mid.astype(jnp.float32)).astype(jnp.bfloat16)
    return hi, mid, lo


def _cumsum_rows(x):
    L = x.shape[0]
    tri = (lax.broadcasted_iota(jnp.int32, (L, L), 0)
           >= lax.broadcasted_iota(jnp.int32, (L, L), 1)).astype(jnp.bfloat16)
    return sum(jnp.dot(tri, p, preferred_element_type=jnp.float32) for p in _split3(x))


def _log_sigmoid(x):
    return jnp.minimum(x, 0.0) - jnp.log(1.0 + jnp.exp(-jnp.abs(x)))


def _softplus(x):
    return jnp.maximum(x, 0.0) + jnp.log(1.0 + jnp.exp(-jnp.abs(x)))


def _mlstm_kernel(qk_ref, v_ref, o_ref, gi_ref, gf_ref, conv0_ref, c0_ref, n0_ref, m0_ref,
                  cw_ref, cb_ref, gb_ref, ng_ref,
                  ha_ref, c1_ref, n1_ref, m1_ref, conv1_ref,
                  buf, c_scr, n_scr, m_scr):
    ck = pl.program_id(1)
    L = qk_ref.shape[0]
    bf16 = jnp.bfloat16

    @pl.when(ck == 0)
    def _():
        buf[5:8, :] = conv0_ref[...]
        c_scr[...] = c0_ref[...]
        n_scr[...] = n0_ref[...]
        m_scr[...] = m0_ref[...]

    qk = _conv_silu(buf, qk_ref[...], cw_ref, cb_ref)
    ig_c = gi_ref[...] + gb_ref[0:1, :]
    lf_c = _log_sigmoid(gf_ref[...] + gb_ref[1:2, :])
    bt_c = _cumsum_rows(lf_c)
    ig_r, bt_r = ig_c.T, bt_c.T
    causal = (lax.broadcasted_iota(jnp.int32, (L, L), 0)
              >= lax.broadcasted_iota(jnp.int32, (L, L), 1))

    for h in range(H_A):
        cs = slice(h * DK_A, (h + 1) * DK_A)
        q = qk[:, cs]
        k = qk[:, D_A + h * DK_A:D_A + (h + 1) * DK_A] * (DK_A ** -0.5)
        qb, kb, vb = q.astype(bf16), k.astype(bf16), v_ref[:, cs].astype(bf16)
        btc, igc = bt_c[:, h:h + 1], ig_c[:, h:h + 1]
        btr, igr = bt_r[h:h + 1, :], ig_r[h:h + 1, :]
        m_old = m_scr[h:h + 1, 0:1]
        C_old, n_old = c_scr[h], n_scr[h:h + 1, :]

        log_d = jnp.where(causal, btc - btr + igr, NEG_INF)
        log_inter = btc + m_old
        m_t = jnp.maximum(log_inter, jnp.max(log_d, axis=1, keepdims=True))
        dmat = jnp.exp(log_d - m_t)
        inter = jnp.exp(log_inter - m_t)
        s = lax.dot_general(qb, kb, (((1,), (1,)), ((), ())),
                            preferred_element_type=jnp.float32) * dmat
        num = (jnp.dot(s.astype(bf16), vb, preferred_element_type=jnp.float32)
               + jnp.dot(qb, C_old.astype(bf16), preferred_element_type=jnp.float32) * inter)
        den = (jnp.sum(s, axis=1, keepdims=True)
               + jnp.sum(q * n_old, axis=1, keepdims=True) * inter)
        hh = num / jnp.maximum(jnp.abs(den), jnp.exp(-m_t))
        hc = hh - jnp.mean(hh, axis=1, keepdims=True)
        hn = hc * lax.rsqrt(jnp.mean(hc * hc, axis=1, keepdims=True) + EPS) * ng_ref[:, cs]
        ha_ref[:, cs] = (hn / (1.0 + jnp.exp(-o_ref[:, cs]))).astype(ha_ref.dtype)

        b_last = btc[L - 1:L, :]
        m_new = jnp.maximum(b_last + m_old, jnp.max(b_last - btr + igr, axis=1, keepdims=True))
        kw = k * jnp.exp(b_last - btc + igc - m_new)
        decay = jnp.exp(b_last + m_old - m_new)
        c_scr[h] = decay * C_old + lax.dot_general(
            kw.astype(bf16), vb, (((0,), (0,)), ((), ())), preferred_element_type=jnp.float32)
        n_scr[h:h + 1, :] = decay * n_old + jnp.sum(kw, axis=0, keepdims=True)
        m_scr[h:h + 1, :] = jnp.broadcast_to(m_new, (1, m_scr.shape[1]))

    @pl.when(ck == pl.num_programs(1) - 1)
    def _():
        c1_ref[...] = c_scr[...]
        n1_ref[...] = n_scr[...]
        m1_ref[...] = m_scr[...]
        conv1_ref[...] = buf[5:8, :]


def mlstm_mixer(big, gates, nseq, T, conv0, C0, n0, m0, conv_w, conv_b, gate_b, norm_g):
    f32 = jnp.float32
    L = CHUNK
    nc = T // L
    m0 = jnp.broadcast_to(m0[:, :, None], (nseq, H_A, 128)).astype(f32)
    gb = jnp.zeros((2, GATE_W), f32).at[:, :H_A].set(gate_b)
    tok = lambda width, col: pl.BlockSpec((L, width), lambda b, c: (b * nc + c, col))
    seq3 = lambda a: pl.BlockSpec((None,) + a.shape[1:], lambda b, c: (b,) + (0,) * (a.ndim - 1))
    full = lambda a: pl.BlockSpec(a.shape, lambda b, c: (0,) * a.ndim)
    cb, ng = conv_b.reshape(1, -1), norm_g.reshape(1, -1)
    out_shapes = [jax.ShapeDtypeStruct((nseq * T, D_A), jnp.bfloat16),
                  jax.ShapeDtypeStruct(C0.shape, f32), jax.ShapeDtypeStruct(n0.shape, f32),
                  jax.ShapeDtypeStruct(m0.shape, f32), jax.ShapeDtypeStruct(conv0.shape, f32)]
    ha, C1, n1, m1, conv1 = pl.pallas_call(
        _mlstm_kernel,
        grid=(nseq, nc),
        in_specs=[tok(2 * D_A, COL_QK), tok(D_A, COL_V), tok(D_A, COL_O),
                  tok(GATE_W, 0), tok(GATE_W, 1),
                  seq3(conv0), seq3(C0), seq3(n0), seq3(m0),
                  full(conv_w), full(cb), full(gb), full(ng)],
        out_specs=[tok(D_A, 0), seq3(C0), seq3(n0), seq3(m0), seq3(conv0)],
        out_shape=out_shapes,
        scratch_shapes=[pltpu.VMEM((8 + L, 2 * D_A), f32), pltpu.VMEM((H_A, DK_A, DV_A), f32),
                        pltpu.VMEM((H_A, DK_A), f32), pltpu.VMEM((H_A, 128), f32)],
        compiler_params=pltpu.CompilerParams(
            dimension_semantics=("arbitrary", "arbitrary"), vmem_limit_bytes=48 * 1024 * 1024),
        name="mlstm_mixer",
    )(big, big, big, gates, gates, conv0, C0, n0, m0, conv_w, cb, gb, ng)
    return ha, (C1, n1, m1[:, :, 0], conv1)


def _ssd_kernel(xs_ref, bc_ref, z_ref, gdt_ref, convx0_ref, convbc0_ref, s0_ref,
                cwx_ref, cbx_ref, cwbc_ref, cbbc_ref, dtb_ref, alog_ref, dskip_ref, ng_ref,
                yb_ref, s1_ref, convx1_ref, convbc1_ref,
                bufx, bufbc, s_scr, y_scr, xw_scr):
    ck = pl.program_id(1)
    L = xs_ref.shape[0]
    bf16 = jnp.bfloat16
    hpg = H_B // G_B
    gw = hpg * P_B

    @pl.when(ck == 0)
    def _():
        bufx[5:8, :] = convx0_ref[...]
        bufbc[5:8, :] = convbc0_ref[...]
        s_scr[...] = s0_ref[...]

    xs = _conv_silu(bufx, xs_ref[...], cwx_ref, cbx_ref)
    bc = _conv_silu(bufbc, bc_ref[...], cwbc_ref, cbbc_ref)
    dt_c = _softplus(gdt_ref[...] + dtb_ref[...])
    cum_c = _cumsum_rows(dt_c * (-jnp.exp(alog_ref[...])))
    cum_r = cum_c.T
    ecum_c = jnp.exp(cum_c)
    causal = (lax.broadcasted_iota(jnp.int32, (L, L), 0)
              >= lax.broadcasted_iota(jnp.int32, (L, L), 1))
    left = lax.broadcasted_iota(jnp.int32, (L, 2 * P_B), 1) < P_B

    def pair_cols(col, h0):
        return jnp.where(left, col[:, h0:h0 + 1], col[:, h0 + 1:h0 + 2])

    for g in range(G_B):
        Bg = bc[:, g * N_STATE:(g + 1) * N_STATE].astype(bf16)
        Cg = bc[:, (G_B + g) * N_STATE:(G_B + g + 1) * N_STATE].astype(bf16)
        cb = lax.dot_general(Cg, Bg, (((1,), (1,)), ((), ())), preferred_element_type=jnp.float32)
        s_old = s_scr[g * gw:(g + 1) * gw, :]
        cs = lax.dot_general(Cg, s_old.astype(bf16), (((1,), (1,)), ((), ())),
                             preferred_element_type=jnp.float32)
        for t in range(hpg // 2):
            h0 = g * hpg + 2 * t
            cols = slice(h0 * P_B, (h0 + 2) * P_B)
            xdt = xs[:, cols] * pair_cols(dt_c, h0)
            xdt_b = xdt.astype(bf16)
            ys = []
            for h in (h0, h0 + 1):
                decay = jnp.exp(jnp.where(causal, cum_c[:, h:h + 1] - cum_r[h:h + 1, :], NEG_INF))
                ys.append(jnp.dot((cb * decay).astype(bf16), xdt_b, preferred_element_type=jnp.float32))
            y_scr[:, cols] = (jnp.where(left, ys[0], ys[1])
                              + cs[:, 2 * t * P_B:(2 * t + 2) * P_B] * pair_cols(ecum_c, h0))
            w_last = jnp.exp(cum_c[L - 1:L, :] - cum_c)
            xw_scr[:, 2 * t * P_B:(2 * t + 2) * P_B] = (xdt * pair_cols(w_last, h0)).astype(bf16)
        upd = lax.dot_general(xw_scr[...], Bg, (((0,), (0,)), ((), ())),
                              preferred_element_type=jnp.float32)
        for hh in range(hpg):
            h = g * hpg + hh
            rows = slice(g * gw + hh * P_B, g * gw + (hh + 1) * P_B)
            s_scr[rows, :] = (ecum_c[L - 1:L, h:h + 1] * s_scr[rows, :]
                              + upd[hh * P_B:(hh + 1) * P_B, :])

    y = (y_scr[...] + dskip_ref[...] * xs) * _silu(z_ref[...])
    yb_ref[...] = (y * lax.rsqrt(jnp.mean(y * y, axis=1, keepdims=True) + EPS)
                   * ng_ref[...]).astype(yb_ref.dtype)

    @pl.when(ck == pl.num_programs(1) - 1)
    def _():
        s1_ref[...] = s_scr[...]
        convx1_ref[...] = bufx[5:8, :]
        convbc1_ref[...] = bufbc[5:8, :]


def ssd_mixer(big, gates, nseq, T, conv0, S0, conv_w, conv_b, dt_bias, a_log, d_skip, norm_g):
    f32 = jnp.float32
    L = CHUNK
    nc = T // L
    nbc = 2 * G_B * N_STATE
    pad = lambda a: jnp.zeros((1, GATE_W), f32).at[0, :H_B].set(a)
    s0 = S0.reshape(nseq, H_B * P_B, N_STATE).astype(f32)
    convx0, convbc0 = conv0[:, :, :D_B], conv0[:, :, D_B:]
    cwx, cwbc = conv_w[:, :D_B], conv_w[:, D_B:]
    cbx, cbbc = conv_b[:D_B].reshape(1, -1), conv_b[D_B:].reshape(1, -1)
    dskip = jnp.repeat(d_skip, P_B).reshape(1, -1)
    tok = lambda width, col: pl.BlockSpec((L, width), lambda b, c: (b * nc + c, col))
    seq3 = lambda a: pl.BlockSpec((None,) + a.shape[1:], lambda b, c: (b,) + (0,) * (a.ndim - 1))
    full = lambda a: pl.BlockSpec(a.shape, lambda b, c: (0,) * a.ndim)
    consts = [cwx, cbx, cwbc, cbbc, pad(dt_bias), pad(a_log), dskip, norm_g.reshape(1, -1)]
    yb, s1, convx1, convbc1 = pl.pallas_call(
        _ssd_kernel,
        grid=(nseq, nc),
        in_specs=[tok(D_B, COL_XS), tok(nbc, COL_BC), tok(D_B, COL_Z), tok(GATE_W, 2),
                  seq3(convx0), seq3(convbc0), seq3(s0)] + [full(a) for a in consts],
        out_specs=[tok(D_B, 0), seq3(s0), seq3(convx0), seq3(convbc0)],
        out_shape=[jax.ShapeDtypeStruct((nseq * T, D_B), jnp.bfloat16),
                   jax.ShapeDtypeStruct(s0.shape, f32),
                   jax.ShapeDtypeStruct(convx0.shape, f32), jax.ShapeDtypeStruct(convbc0.shape, f32)],
        scratch_shapes=[pltpu.VMEM((8 + L, D_B), f32), pltpu.VMEM((8 + L, nbc), f32),
                        pltpu.VMEM((H_B * P_B, N_STATE), f32), pltpu.VMEM((L, D_B), f32),
                        pltpu.VMEM((L, H_B // G_B * P_B), jnp.bfloat16)],
        compiler_params=pltpu.CompilerParams(
            dimension_semantics=("arbitrary", "arbitrary"), vmem_limit_bytes=48 * 1024 * 1024),
        name="ssd_mixer",
    )(big, big, big, gates, convx0, convbc0, s0, *consts)
    return yb, (s1.reshape(S0.shape), jnp.concatenate([convx1, convbc1], axis=-1))


def _decode_mixers(big, gates, bsz, st, wl):
    (_, _, _, _, gate_b, conv_a_w, conv_a_b, mlstm_norm_g,
     conv_b_w, conv_b_b, dt_bias, a_log, d_skip, ssm_norm_g, _, _) = wl
    C0, n0, m0, conv_a0, S0, conv_b0 = st
    f32 = jnp.float32
    T = chunk = 1
    big = big.reshape(bsz, T, -1)
    gates = gates.reshape(bsz, T, -1)
    q, k, v, o, z = (big[..., i * D_MODEL:(i + 1) * D_MODEL] for i in range(5))
    xbc = big[..., 5 * D_MODEL:N_BIG]
    ig, fg = gates[..., :H_A], gates[..., GATE_W:GATE_W + H_A]
    dt = gates[..., 2 * GATE_W:2 * GATE_W + H_B]

    qk, conv_a1 = causal_conv_silu(jnp.concatenate([q, k], axis=-1), conv_a0, conv_a_w, conv_a_b)
    q, k = jnp.split(qk, 2, axis=-1)
    qh = q.reshape(bsz, T, H_A, DK_A).astype(f32)
    kh = k.reshape(bsz, T, H_A, DK_A).astype(f32) * (DK_A ** -0.5)
    vh = v.reshape(bsz, T, H_A, DV_A).astype(f32)
    log_i = ig.astype(f32) + gate_b[0].astype(f32)
    log_f = jax.nn.log_sigmoid(fg.astype(f32) + gate_b[1].astype(f32))
    xs_a = tuple(to_chunks(a, chunk) for a in (qh, kh, vh, log_i, log_f))
    (C1, n1, m1), h_a = lax.scan(mlstm_chunk, (C0.astype(f32), n0.astype(f32), m0.astype(f32)), xs_a)
    h_a = head_layernorm(from_chunks(h_a), mlstm_norm_g)
    h_a = h_a * jax.nn.sigmoid(o.astype(f32))

    xbc, conv_b1 = causal_conv_silu(xbc, conv_b0, conv_b_w, conv_b_b)
    xs_b, Bm, Cm = jnp.split(xbc, [D_B, D_B + G_B * N_STATE], axis=-1)
    xh = xs_b.reshape(bsz, T, H_B, P_B).astype(f32)
    Bm = Bm.reshape(bsz, T, G_B, N_STATE).astype(f32)
    Cm = Cm.reshape(bsz, T, G_B, N_STATE).astype(f32)
    dt = jax.nn.softplus(dt.astype(f32) + dt_bias.astype(f32))
    A = -jnp.exp(a_log.astype(f32))
    xs_s = tuple(to_chunks(a, chunk) for a in (xh, dt, Bm, Cm))
    S1, y_b = lax.scan(functools.partial(ssd_chunk, A), S0.astype(f32), xs_s)
    y_b = from_chunks(y_b) + d_skip.astype(f32)[:, None] * xh
    y_b = y_b.reshape(bsz, T, D_B) * jax.nn.silu(z.astype(f32))
    y_b = rmsnorm(y_b, ssm_norm_g)
    bf16 = jnp.bfloat16
    return (h_a.reshape(bsz, -1).astype(bf16), (C1, n1, m1, conv_a1),
            y_b.reshape(bsz, -1).astype(bf16), (S1, conv_b1))


def hybrid_layer(x, mod, nseq, T, st, wl):
    (norm1_g, norm2_g, w_big, w_gate, gate_b, conv_a_w, conv_a_b, mlstm_norm_g,
     conv_b_w, conv_b_b, dt_bias, a_log, d_skip, ssm_norm_g, w_out, peer_w) = wl
    C0, n0, m0, conv_a0, S0, conv_b0 = st
    sh1, sc1, g1, sh2, sc2, g2 = jnp.split(mod, 6, axis=-1)
    big, gates = in_projection(x, sc1, sh1, T, norm1_g, w_big, w_gate)
    if T % CHUNK == 0:
        ha, st_a = mlstm_mixer(big, gates, nseq, T, conv_a0, C0, n0, m0,
                               conv_a_w, conv_a_b, gate_b, mlstm_norm_g)
        yb, st_b = ssd_mixer(big, gates, nseq, T, conv_b0, S0,
                             conv_b_w, conv_b_b, dt_bias, a_log, d_skip, ssm_norm_g)
    else:
        ha, st_a, yb, st_b = _decode_mixers(big, gates, nseq, st, wl)
    x1, h2t = out_projection(ha, yb, w_out, x, g1, sc2, sh2, T, norm2_g)
    x2 = peer_residual(x1, h2t, g2, T, *peer_w)
    return x2, st_a + st_b


def _final_norm_kernel(x_ref, g_ref, o_ref):
    x = x_ref[...]
    y = x * lax.rsqrt(jnp.mean(x * x, axis=-1, keepdims=True) + EPS)
    o_ref[...] = y * g_ref[...]


def _final_norm(x, g):
    shp = x.shape
    x2 = x.reshape(-1, shp[-1])
    n = x2.shape[0]
    tm = min(n, 256)
    out = pl.pallas_call(
        _final_norm_kernel,
        grid=(n // tm,),
        in_specs=[pl.BlockSpec((tm, shp[-1]), lambda i: (i, 0)),
                  pl.BlockSpec((1, shp[-1]), lambda i: (0, 0))],
        out_specs=pl.BlockSpec((tm, shp[-1]), lambda i: (i, 0)),
        out_shape=jax.ShapeDtypeStruct(x2.shape, x2.dtype),
    )(x2, g.reshape(1, -1))
    return out.reshape(shp)


def kernel(x_prompt, x_sample, c_prompt, c_sample, state_mlstm_c, state_mlstm_n, state_mlstm_m, state_mlstm_conv, state_ssm, state_ssm_conv, w_ada, b_ada, norm1_g, norm2_g, w_in, mlstm_gate_b, conv_a_w, conv_a_b, mlstm_norm_g, conv_b_w, conv_b_b, dt_bias, a_log, d_skip, ssm_norm_g, w_out, peer_w_query, peer_sub_keys, peer_u, peer_v, final_norm_g):
    f32, bf16 = jnp.float32, jnp.bfloat16
    bp = x_prompt.shape[0]
    zero_state = (jnp.zeros((bp, H_A, DK_A, DV_A), f32),
                  jnp.zeros((bp, H_A, DK_A), f32),
                  jnp.zeros((bp, H_A), f32),
                  jnp.zeros((bp, CONV_W - 1, D_CONV_A), x_prompt.dtype),
                  jnp.zeros((bp, H_B, P_B, N_STATE), f32),
                  jnp.zeros((bp, CONV_W - 1, D_CONV_B), x_prompt.dtype))
    bs, ts = x_sample.shape[:2]
    tp = x_prompt.shape[1]
    hp, hs = x_prompt.reshape(bp * tp, -1), x_sample.reshape(bs * ts, -1)
    c_all = jnp.concatenate([c_prompt, c_sample], axis=0)
    t_e = PEER_EXPERT_TILE
    new_p, new_s = [], []
    for l in range(DEPTH):
        wi = w_in[l]
        lo = np.cumsum((0,) + SPLIT_SIZES)
        w_big = jnp.concatenate([wi[:, lo[0]:lo[4]], wi[:, lo[6]:lo[8]]], axis=1).astype(bf16)
        w_gate = jnp.zeros((D_MODEL, 3, GATE_W), f32)
        w_gate = w_gate.at[:, 0, :H_A].set(wi[:, lo[4]:lo[5]]).at[:, 1, :H_A].set(wi[:, lo[5]:lo[6]])
        w_gate = w_gate.at[:, 2, :H_B].set(wi[:, lo[8]:lo[9]]).reshape(D_MODEL, 3 * GATE_W).astype(bf16)
        peer_w = (peer_w_query[l].T.astype(bf16), peer_sub_keys[l].astype(bf16), peer_u[l].astype(bf16),
                  peer_v[l].astype(bf16).reshape(N_EXPERTS // t_e, t_e, D_MODEL).transpose(0, 2, 1))
        wl = (norm1_g[l], norm2_g[l], w_big, w_gate, mlstm_gate_b[l], conv_a_w[l],
              conv_a_b[l], mlstm_norm_g[l], conv_b_w[l], conv_b_b[l], dt_bias[l], a_log[l],
              d_skip[l], ssm_norm_g[l], w_out[l].astype(bf16), peer_w)
        mod = adaln(c_all, w_ada[l], b_ada[l])
        hp, st_p = hybrid_layer(hp, mod[:bp], bp, tp, zero_state, wl)
        st_in = (state_mlstm_c[l], state_mlstm_n[l], state_mlstm_m[l], state_mlstm_conv[l],
                 state_ssm[l], state_ssm_conv[l])
        hs, st_s = hybrid_layer(hs, mod[bp:], bs, ts, st_in, wl)
        new_p.append(st_p)
        new_s.append(st_s)
    y_prompt = _final_norm(hp, final_norm_g).reshape(x_prompt.shape)
    y_sample = _final_norm(hs, final_norm_g).reshape(x_sample.shape)
    dtypes = (state_mlstm_c.dtype, state_mlstm_n.dtype, state_mlstm_m.dtype,
              state_mlstm_conv.dtype, state_ssm.dtype, state_ssm_conv.dtype)

    def stacked(states, i):
        return jnp.stack([s[i] for s in states], axis=0).astype(dtypes[i])

    return (y_prompt, y_sample,
            stacked(new_p, 0), stacked(new_p, 1), stacked(new_p, 2),
            stacked(new_p, 3), stacked(new_p, 4), stacked(new_p, 5),
            stacked(new_s, 0), stacked(new_s, 1), stacked(new_s, 2),
            stacked(new_s, 3), stacked(new_s, 4), stacked(new_s, 5))
```
